```python
import jax, jax.numpy as jnp
from jax import lax
import numpy as np

D_MODEL = 2048
BATCH = 8
SEQ = 2048
DEPTH = 4
DEC_BATCH = 8
DEC_SEQ = 32
PAST_LEN = 2048

CHUNK = 64
D_A = 1024
D_B = 1024
D_MIX = D_A + D_B
P_IN = 2 * D_A + 3 * D_B
WIDTH_A = 31
WIDTH_B = 3
WIDTH_FFN = 3
D_FF = 5632
EPS = 1e-6

kernel_name = "hybrid_conformer_shortconv_streaming_step"


def rms_norm(x, g):
    xf = x.astype(jnp.float32)
    y = xf * lax.rsqrt(jnp.mean(xf * xf, axis=-1, keepdims=True) + EPS)
    return (y * g.astype(jnp.float32)).astype(x.dtype)


def layer_norm(x, g, b):
    xf = x.astype(jnp.float32)
    mu = jnp.mean(xf, axis=-1, keepdims=True)
    var = jnp.mean(jnp.square(xf - mu), axis=-1, keepdims=True)
    y = (xf - mu) * lax.rsqrt(var + EPS)
    return (y * g.astype(jnp.float32) + b.astype(jnp.float32)).astype(x.dtype)


def causal_dwconv(hist, x, w):
    k = w.shape[0]
    xp = jnp.concatenate([hist.astype(x.dtype), x], axis=1)
    y = lax.conv_general_dilated(
        xp, w[:, None, :].astype(x.dtype), window_strides=(1,), padding='VALID',
        dimension_numbers=('NWC', 'WIO', 'NWC'), feature_group_count=x.shape[-1])
    return y, xp[:, xp.shape[1] - (k - 1):, :]


def trunk_layer(x, h_a, h_b, h_f, norm1_g, w_in, conv_a_w, conv_a_b, ln_a_g, ln_a_b,
                conv_b_w, w_out, norm2_g, w_up, conv_ffn_w, w_down):
    h = rms_norm(x, norm1_g)
    z = jnp.einsum('btd,dp->btp', h, w_in)
    a_val, a_gate, b_b, b_c, b_h = jnp.split(
        z, [D_A, 2 * D_A, 2 * D_A + D_B, 2 * D_A + 2 * D_B], axis=-1)
    a = a_val * jax.nn.sigmoid(a_gate)
    a, new_a = causal_dwconv(h_a, a, conv_a_w)
    a = jax.nn.silu(layer_norm(a + conv_a_b, ln_a_g, ln_a_b))
    u, new_b = causal_dwconv(h_b, b_c * b_h, conv_b_w)
    y_b = b_b * u
    x = x + jnp.einsum('btm,md->btd', jnp.concatenate([a, y_b], axis=-1), w_out)
    h = rms_norm(x, norm2_g)
    u = jnp.einsum('btd,df->btf', h, w_up)
    u, new_f = causal_dwconv(h_f, u, conv_ffn_w)
    g, v = jnp.split(u, 2, axis=-1)
    x = x + jnp.einsum('btf,fd->btd', jax.nn.silu(g) * v, w_down)
    return x, new_a, new_b, new_f


def setup_inputs(seed: int = 0) -> dict:
    key = jax.random.key(seed)
    ks = jax.random.split(key, 20)
    f32 = jnp.float32
    nrm = lambda k, shape, s: (jax.random.normal(k, shape, f32) * s).astype(f32)
    return {
        "x_prompt": nrm(ks[0], (BATCH, SEQ, D_MODEL), 1.0),
        "x_sample": nrm(ks[1], (DEC_BATCH, DEC_SEQ, D_MODEL), 1.0),
        "state_conv_a": nrm(ks[2], (DEPTH, DEC_BATCH, WIDTH_A - 1, D_A), 0.5),
        "state_conv_b": nrm(ks[3], (DEPTH, DEC_BATCH, WIDTH_B - 1, D_B), 0.5),
        "state_ffn": nrm(ks[4], (DEPTH, DEC_BATCH, WIDTH_FFN - 1, 2 * D_FF), 1.0),
        "norm1_g": 1.0 + nrm(ks[5], (DEPTH, D_MODEL), 0.01),
        "w_in": nrm(ks[6], (DEPTH, D_MODEL, P_IN), D_MODEL ** -0.5),
        "conv_a_w": nrm(ks[7], (DEPTH, WIDTH_A, D_A), WIDTH_A ** -0.5),
        "conv_a_b": nrm(ks[8], (DEPTH, D_A), 0.01),
        "ln_a_g": 1.0 + nrm(ks[9], (DEPTH, D_A), 0.01),
        "ln_a_b": nrm(ks[10], (DEPTH, D_A), 0.01),
        "conv_b_w": nrm(ks[11], (DEPTH, WIDTH_B, D_B), WIDTH_B ** -0.5),
        "w_out": nrm(ks[12], (DEPTH, D_MIX, D_MODEL), D_MIX ** -0.5),
        "norm2_g": 1.0 + nrm(ks[13], (DEPTH, D_MODEL), 0.01),
        "w_up": nrm(ks[14], (DEPTH, D_MODEL, 2 * D_FF), D_MODEL ** -0.5),
        "conv_ffn_w": nrm(ks[15], (DEPTH, WIDTH_FFN, 2 * D_FF), WIDTH_FFN ** -0.5),
        "w_down": nrm(ks[16], (DEPTH, D_FF, D_MODEL), D_FF ** -0.5),
        "final_g": 1.0 + nrm(ks[17], (D_MODEL,), 0.01),
    }


def reference(x_prompt, x_sample, state_conv_a, state_conv_b, state_ffn,
              norm1_g, w_in, conv_a_w, conv_a_b, ln_a_g, ln_a_b, conv_b_w, w_out,
              norm2_g, w_up, conv_ffn_w, w_down, final_g):
    bp = x_prompt.shape[0]
    dt = x_prompt.dtype
    xp, xs = x_prompt, x_sample
    pa, pb, pf, sa, sb, sf = [], [], [], [], [], []
    for l in range(DEPTH):
        lw = (norm1_g[l], w_in[l], conv_a_w[l], conv_a_b[l], ln_a_g[l], ln_a_b[l],
              conv_b_w[l], w_out[l], norm2_g[l], w_up[l], conv_ffn_w[l], w_down[l])
        xp, na, nb, nf = trunk_layer(
            xp,
            jnp.zeros((bp, WIDTH_A - 1, D_A), dt),
            jnp.zeros((bp, WIDTH_B - 1, D_B), dt),
            jnp.zeros((bp, WIDTH_FFN - 1, 2 * D_FF), dt),
            *lw)
        pa.append(na); pb.append(nb); pf.append(nf)
        xs, na, nb, nf = trunk_layer(xs, state_conv_a[l], state_conv_b[l], state_ffn[l], *lw)
        sa.append(na); sb.append(nb); sf.append(nf)
    y_prompt = rms_norm(xp, final_g)
    y_sample = rms_norm(xs, final_g)
    new_conv_a_prompt = jnp.stack(pa, axis=0)
    new_conv_b_prompt = jnp.stack(pb, axis=0)
    new_ffn_prompt = jnp.stack(pf, axis=0)
    new_conv_a_sample = jnp.stack(sa, axis=0)
    new_conv_b_sample = jnp.stack(sb, axis=0)
    new_ffn_sample = jnp.stack(sf, axis=0)
    return (y_prompt, y_sample, new_conv_a_prompt, new_conv_b_prompt, new_ffn_prompt,
            new_conv_a_sample, new_conv_b_sample, new_ffn_sample)
```

```python
import jax
import jax.numpy as jnp
from jax import lax
from jax.experimental import pallas as pl
from jax.experimental.pallas import tpu as pltpu

F32 = jnp.float32
BF16 = jnp.bfloat16
EPS = 1e-6

V7X_SUBLANES = 8
V7X_LANES = 128
V7X_VMEM_LIMIT_BYTES = 56 * 1024 * 1024

CONV_A_PAD = 32
CONV3_PAD = V7X_SUBLANES
CONV_ROWS = 32


def _sigmoid(x):
    return 1.0 / (1.0 + jnp.exp(-x))


def _rms_norm_rows(x, g):
    y = x * lax.rsqrt(jnp.mean(x * x, axis=-1, keepdims=True) + EPS)
    return y * g


def _lanes(c):
    return slice(c * V7X_LANES, (c + 1) * V7X_LANES)


def _params(n_axes):
    return pltpu.CompilerParams(
        dimension_semantics=("arbitrary",) * n_axes,
        vmem_limit_bytes=V7X_VMEM_LIMIT_BYTES,
    )


def _resident(block_shape, index_map):
    return pl.BlockSpec(block_shape, index_map, pipeline_mode=pl.Buffered(1))


def _conv_taps(pad_ref, lead, first_row, rows, w_ref, c):
    acc = None
    for k in range(w_ref.shape[0]):
        win = pad_ref[lead + (pl.ds(first_row + k, rows), slice(None))]
        term = win * w_ref[pl.ds(k, 1), _lanes(c)]
        acc = term if acc is None else acc + term
    return acc


def _inproj_kernel(x_ref, g_ref, wav_ref, wag_ref, wbb_ref, wbc_ref, wbh_ref,
                   glu_ref, cbh_ref, bb_ref, h_ref):
    sb, tm, d = x_ref.shape
    cb = glu_ref.shape[-1]
    rows = sb * tm

    @pl.when(pl.program_id(2) == 0)
    def _():
        x = x_ref[...].reshape(rows, d)
        h_ref[...] = _rms_norm_rows(x, g_ref[...]).astype(BF16)

    h = h_ref[...]

    def proj(w_ref):
        return jnp.dot(h, w_ref[...], preferred_element_type=F32)

    glu = proj(wav_ref) * _sigmoid(proj(wag_ref))
    glu_ref[...] = glu.reshape(sb, tm, cb)
    cbh_ref[...] = (proj(wbc_ref) * proj(wbh_ref)).reshape(sb, tm, cb)
    bb_ref[...] = proj(wbb_ref).reshape(sb, tm, cb)


def _inproj(x, norm_g, w_in, layer, sb, tm, cb):
    s, t, d = x.shape
    d_a = w_in.shape[-1] // 5
    nj = d_a // cb
    grid = (s // sb, t // tm, nj)

    def wspec(group):
        return pl.BlockSpec((None, d, cb), lambda i, k, j: (layer, 0, group * nj + j))

    act_spec = pl.BlockSpec((sb, tm, cb), lambda i, k, j: (i, k, j))
    out_sds = jax.ShapeDtypeStruct((s, t, d_a), F32)
    return pl.pallas_call(
        _inproj_kernel,
        grid=grid,
        in_specs=[
            pl.BlockSpec((sb, tm, d), lambda i, k, j: (i, k, 0)),
            pl.BlockSpec((None, 1, d), lambda i, k, j: (layer, 0, 0)),
            wspec(0), wspec(1), wspec(2), wspec(3), wspec(4),
        ],
        out_specs=[act_spec, act_spec, act_spec],
        out_shape=[out_sds, out_sds, out_sds],
        scratch_shapes=[pltpu.VMEM((sb * tm, d), BF16)],
        compiler_params=_params(3),
        name="inproj",
    )(x, norm_g, w_in, w_in, w_in, w_in, w_in)


def _mixout_kernel(x_ref, glu_ref, cbh_ref, bb_ref, hista_ref, histb_ref,
                   wa_ref, ba_ref, lng_ref, lnb_ref, wb_ref, wout_ref,
                   o_ref, pada_ref, padb_ref, mix_ref):
    sb, tm, d = x_ref.shape
    d_a = glu_ref.shape[-1]
    d_b = cbh_ref.shape[-1]
    nca = d_a // V7X_LANES
    ncb = d_b // V7X_LANES
    a0 = CONV_A_PAD - (wa_ref.shape[0] - 1)
    b0 = CONV3_PAD - (wb_ref.shape[0] - 1)

    @pl.when(pl.program_id(1) == 0)
    def _():
        for c in range(nca):
            pada_ref[:, c, 0:CONV_A_PAD, :] = hista_ref[:, :, _lanes(c)]
        for c in range(ncb):
            padb_ref[:, c, 0:CONV3_PAD, :] = histb_ref[:, :, _lanes(c)]

    for c in range(nca):
        pada_ref[:, c, CONV_A_PAD:, :] = glu_ref[:, :, _lanes(c)]
    for c in range(ncb):
        padb_ref[:, c, CONV3_PAD:, :] = cbh_ref[:, :, _lanes(c)]

    rc = min(CONV_ROWS, tm)

    def chunk(s_idx, r0):
        y = jnp.concatenate(
            [_conv_taps(pada_ref, (s_idx, c), r0 + a0, rc, wa_ref, c) for c in range(nca)],
            axis=-1) + ba_ref[...]
        mu = jnp.mean(y, axis=-1, keepdims=True)
        yc = y - mu
        var = jnp.mean(yc * yc, axis=-1, keepdims=True)
        yn = yc * lax.rsqrt(var + EPS) * lng_ref[...] + lnb_ref[...]
        row0 = pl.multiple_of(s_idx * tm + r0, rc)
        mix_ref[pl.ds(row0, rc), 0:d_a] = (yn * _sigmoid(yn)).astype(BF16)

        u = jnp.concatenate(
            [_conv_taps(padb_ref, (s_idx, c), r0 + b0, rc, wb_ref, c) for c in range(ncb)],
            axis=-1)
        yb = bb_ref[s_idx, pl.ds(r0, rc), :] * u
        mix_ref[pl.ds(row0, rc), d_a:] = yb.astype(BF16)

    for s_idx in range(sb):
        if tm == rc:
            chunk(s_idx, 0)
        else:
            def body(i, carry, s_idx=s_idx):
                chunk(s_idx, pl.multiple_of(i * rc, rc))
                return carry
            lax.fori_loop(0, tm // rc, body, 0)

    y = jnp.dot(mix_ref[...], wout_ref[...], preferred_element_type=F32)
    o_ref[...] = x_ref[...] + y.reshape(sb, tm, d)

    pada_ref[:, :, 0:CONV_A_PAD, :] = pada_ref[:, :, tm:tm + CONV_A_PAD, :]
    padb_ref[:, :, 0:CONV3_PAD, :] = padb_ref[:, :, tm:tm + CONV3_PAD, :]


def _mixout(x, glu, cbh, bb, hist_a, hist_b, conv_a_w, conv_a_b, ln_g, ln_b,
            conv_b_w, w_out, layer, sb, tm):
    s, t, d = x.shape
    d_a = glu.shape[-1]
    d_b = cbh.shape[-1]
    ka = conv_a_w.shape[1]
    kb = conv_b_w.shape[1]
    grid = (s // sb, t // tm)
    tile = lambda c: pl.BlockSpec((sb, tm, c), lambda i, k: (i, k, 0))
    vec = lambda c: _resident((None, 1, c), lambda i, k: (layer, 0, 0))
    return pl.pallas_call(
        _mixout_kernel,
        grid=grid,
        in_specs=[
            tile(d), tile(d_a), tile(d_b), tile(d_b),
            pl.BlockSpec((sb, CONV_A_PAD, d_a), lambda i, k: (i, 0, 0)),
            pl.BlockSpec((sb, CONV3_PAD, d_b), lambda i, k: (i, 0, 0)),
            _resident((None, ka, d_a), lambda i, k: (layer, 0, 0)),
            vec(d_a), vec(d_a), vec(d_a),
            _resident((None, kb, d_b), lambda i, k: (layer, 0, 0)),
            _resident((None, d_a + d_b, d), lambda i, k: (layer, 0, 0)),
        ],
        out_specs=tile(d),
        out_shape=jax.ShapeDtypeStruct((s, t, d), F32),
        scratch_shapes=[
            pltpu.VMEM((sb, d_a // V7X_LANES, CONV_A_PAD + tm, V7X_LANES), F32),
            pltpu.VMEM((sb, d_b // V7X_LANES, CONV3_PAD + tm, V7X_LANES), F32),
            pltpu.VMEM((sb * tm, d_a + d_b), BF16),
        ],
        compiler_params=_params(2),
        name="mixout",
    )(x, glu, cbh, bb, hist_a, hist_b, conv_a_w, conv_a_b, ln_g, ln_b, conv_b_w, w_out)


def _ffn_kernel(x_ref, g_ref, wg_ref, wv_ref, cwg_ref, cwv_ref, histg_ref, histv_ref,
                wd_ref, o_ref, newg_ref, newv_ref, h_ref, pad_ref, carry_ref, act_ref):
    sb, tm, d = x_ref.shape
    fb = wg_ref.shape[-1]
    nc = fb // V7X_LANES
    kf = cwg_ref.shape[0]
    rows = sb * tm
    p0 = CONV3_PAD - (kf - 1)
    t_idx = pl.program_id(1)
    j = pl.program_id(2)

    @pl.when(j == 0)
    def _():
        x = x_ref[...].reshape(rows, d)
        h_ref[...] = _rms_norm_rows(x, g_ref[...]).astype(BF16)

    @pl.when(t_idx == 0)
    def _():
        for c in range(nc):
            pad_ref[:, c, 0:CONV3_PAD, :] = histg_ref[:, :, _lanes(c)]
            pad_ref[:, nc + c, 0:CONV3_PAD, :] = histv_ref[:, :, _lanes(c)]

    @pl.when(t_idx > 0)
    def _():
        pad_ref[:, :, 0:CONV3_PAD, :] = carry_ref[j]

    h = h_ref[...]
    ug = jnp.dot(h, wg_ref[...], preferred_element_type=F32).reshape(sb, tm, fb)
    uv = jnp.dot(h, wv_ref[...], preferred_element_type=F32).reshape(sb, tm, fb)
    for c in range(nc):
        pad_ref[:, c, CONV3_PAD:, :] = ug[:, :, _lanes(c)]
        pad_ref[:, nc + c, CONV3_PAD:, :] = uv[:, :, _lanes(c)]

    carry_ref[j] = pad_ref[:, :, tm:tm + CONV3_PAD, :]
    for c in range(nc):
        newg_ref[:, :, _lanes(c)] = pad_ref[:, c, tm + p0:tm + CONV3_PAD, :]
        newv_ref[:, :, _lanes(c)] = pad_ref[:, nc + c, tm + p0:tm + CONV3_PAD, :]

    rc = min(2 * CONV_ROWS, tm)
    for s_idx in range(sb):
        for r0 in range(0, tm, rc):
            for c in range(nc):
                cg = _conv_taps(pad_ref, (s_idx, c), r0 + p0, rc, cwg_ref, c)
                cv = _conv_taps(pad_ref, (s_idx, nc + c), r0 + p0, rc, cwv_ref, c)
                row0 = s_idx * tm + r0
                act_ref[row0:row0 + rc, _lanes(c)] = (cg * _sigmoid(cg) * cv).astype(BF16)

    y = jnp.dot(act_ref[...], wd_ref[...], preferred_element_type=F32).reshape(sb, tm, d)

    @pl.when(j == 0)
    def _():
        o_ref[...] = x_ref[...] + y

    @pl.when(j > 0)
    def _():
        o_ref[...] += y


def _ffn(x, norm_g, w_up, conv_w, hist, w_down, layer, sb, tm, fb):
    s, t, d = x.shape
    d_ff = w_down.shape[1]
    kf = conv_w.shape[1]
    nf = d_ff // fb
    grid = (s // sb, t // tm, nf)
    tile = pl.BlockSpec((sb, tm, d), lambda i, k, j: (i, k, 0))
    new_spec = pl.BlockSpec((sb, None, kf - 1, fb), lambda i, k, j: (i, k, 0, j))
    new_sds = jax.ShapeDtypeStruct((s, t // tm, kf - 1, d_ff), F32)
    x_new, new_g, new_v = pl.pallas_call(
        _ffn_kernel,
        grid=grid,
        in_specs=[
            tile,
            pl.BlockSpec((None, 1, d), lambda i, k, j: (layer, 0, 0)),
            pl.BlockSpec((None, d, fb), lambda i, k, j: (layer, 0, j)),
            pl.BlockSpec((None, d, fb), lambda i, k, j: (layer, 0, nf + j)),
            pl.BlockSpec((None, kf, fb), lambda i, k, j: (layer, 0, j)),
            pl.BlockSpec((None, kf, fb), lambda i, k, j: (layer, 0, nf + j)),
            pl.BlockSpec((sb, CONV3_PAD, fb), lambda i, k, j: (i, 0, j)),
            pl.BlockSpec((sb, CONV3_PAD, fb), lambda i, k, j: (i, 0, nf + j)),
            pl.BlockSpec((None, fb, d), lambda i, k, j: (layer, j, 0)),
        ],
        out_specs=[tile, new_spec, new_spec],
        out_shape=[jax.ShapeDtypeStruct((s, t, d), F32), new_sds, new_sds],
        scratch_shapes=[
            pltpu.VMEM((sb * tm, d), BF16),
            pltpu.VMEM((sb, 2 * fb // V7X_LANES, CONV3_PAD + tm, V7X_LANES), F32),
            pltpu.VMEM((nf, sb, 2 * fb // V7X_LANES, CONV3_PAD, V7X_LANES), F32),
            pltpu.VMEM((sb * tm, fb), BF16),
        ],
        compiler_params=_params(3),
        name="convffn",
    )(x, norm_g, w_up, w_up, conv_w, conv_w, hist, hist, w_down)
    return x_new, jnp.concatenate([new_g[:, -1], new_v[:, -1]], axis=-1)


def _final_norm_kernel(x_ref, g_ref, o_ref):
    sb, tm, d = x_ref.shape
    x = x_ref[...].reshape(sb * tm, d)
    o_ref[...] = _rms_norm_rows(x, g_ref[...]).reshape(sb, tm, d)


def _final_norm(x, g, sb, tm):
    s, t, d = x.shape
    tile = pl.BlockSpec((sb, tm, d), lambda i, k: (i, k, 0))
    return pl.pallas_call(
        _final_norm_kernel,
        grid=(s // sb, t // tm),
        in_specs=[tile, _resident((1, d), lambda i, k: (0, 0))],
        out_specs=tile,
        out_shape=jax.ShapeDtypeStruct((s, t, d), F32),
        compiler_params=_params(2),
        name="finalnorm",
    )(x, g)


def _tiling(s, t):
    tm = min(t, 512)
    sb = s if s * t <= 512 else 1
    return sb, tm


def _front_pad(state, rows):
    return jnp.pad(state, ((0, 0), (0, 0), (rows - state.shape[2], 0), (0, 0)))


def _trunk(x, state_a, state_b, state_f, p):
    s, t, d = x.shape
    sb, tm = _tiling(s, t)
    depth = p["w_in"].shape[0]
    ka1, kb1 = state_a.shape[2], state_b.shape[2]
    hist_a = _front_pad(state_a, CONV_A_PAD)
    hist_b = _front_pad(state_b, CONV3_PAD)
    hist_f = _front_pad(state_f, CONV3_PAD)
    new_a, new_b, new_f = [], [], []
    for layer in range(depth):
        glu, cbh, bb = _inproj(x, p["norm1_g"], p["w_in"], layer, sb, tm, cb=256)
        new_a.append(glu[:, t - ka1:, :])
        new_b.append(cbh[:, t - kb1:, :])
        x = _mixout(x, glu, cbh, bb, hist_a[layer], hist_b[layer],
                    p["conv_a_w"], p["conv_a_b"], p["ln_a_g"], p["ln_a_b"],
                    p["conv_b_w"], p["w_out"], layer, sb, tm)
        x, nf = _ffn(x, p["norm2_g"], p["w_up"], p["conv_ffn_w"], hist_f[layer],
                     p["w_down"], layer, sb, tm, fb=512)
        new_f.append(nf)
    y = _final_norm(x, p["final_g"], sb, tm)
    return y, jnp.stack(new_a), jnp.stack(new_b), jnp.stack(new_f)


def kernel(x_prompt, x_sample, state_conv_a, state_conv_b, state_ffn, norm1_g, w_in,
           conv_a_w, conv_a_b, ln_a_g, ln_a_b, conv_b_w, w_out, norm2_g, w_up,
           conv_ffn_w, w_down, final_g):
    depth = w_in.shape[0]
    vec = lambda a: a.reshape(depth, 1, a.shape[-1])
    p = dict(
        norm1_g=vec(norm1_g), w_in=w_in.astype(BF16),
        conv_a_w=conv_a_w, conv_a_b=vec(conv_a_b), ln_a_g=vec(ln_a_g), ln_a_b=vec(ln_a_b),
        conv_b_w=conv_b_w, w_out=w_out.astype(BF16),
        norm2_g=vec(norm2_g), w_up=w_up.astype(BF16), conv_ffn_w=conv_ffn_w,
        w_down=w_down.astype(BF16), final_g=final_g.reshape(1, -1),
    )
    bp = x_prompt.shape[0]
    no_history = lambda st: jnp.zeros((depth, bp) + st.shape[2:], st.dtype)
    yp, pa, pb, pf = _trunk(x_prompt, no_history(state_conv_a), no_history(state_conv_b),
                            no_history(state_ffn), p)
    ys, sa, sbb, sf = _trunk(x_sample, state_conv_a, state_conv_b, state_ffn, p)
    return yp, ys, pa, pb, pf, sa, sbb, sf
```

```python
import jax
import jax.numpy as jnp
from jax import lax
from jax.experimental import pallas as pl
from jax.experimental.pallas import tpu as pltpu

F32 = jnp.float32
BF16 = jnp.bfloat16
EPS = 1e-6

V7X_SUBLANES = 8
V7X_LANES = 128
V7X_VMEM_LIMIT_BYTES = 56 * 1024 * 1024

CONV_A_PAD = 32
CONV3_PAD = V7X_SUBLANES
CONV_ROWS = 32


def _sigmoid(x):
    return 1.0 / (1.0 + jnp.exp(-x))


def _rms_norm_rows(x, g):
    y = x * lax.rsqrt(jnp.mean(x * x, axis=-1, keepdims=True) + EPS)
    return y * g


def _lanes(c):
    return slice(c * V7X_LANES, (c + 1) * V7X_LANES)


def _params(n_axes, flags=None):
    return pltpu.CompilerParams(
        dimension_semantics=("arbitrary",) * n_axes,
        vmem_limit_bytes=V7X_VMEM_LIMIT_BYTES,
        flags=flags,
    )


def _resident(block_shape, index_map):
    return pl.BlockSpec(block_shape, index_map, pipeline_mode=pl.Buffered(1))


def _conv_taps(pad_ref, lead, first_row, rows, w_ref, c):
    acc = None
    for k in range(w_ref.shape[0]):
        win = pad_ref[lead + (pl.ds(first_row + k, rows), slice(None))]
        term = win * w_ref[pl.ds(k, 1), _lanes(c)]
        acc = term if acc is None else acc + term
    return acc


def _inproj_kernel(x_ref, g_ref, wav_ref, wag_ref, wbb_ref, wbc_ref, wbh_ref,
                   glu_ref, cbh_ref, bb_ref, h_ref):
    sb, tm, d = x_ref.shape
    cb = glu_ref.shape[-1]
    rows = sb * tm

    @pl.when(pl.program_id(2) == 0)
    def _():
        x = x_ref[...].reshape(rows, d)
        h_ref[...] = _rms_norm_rows(x, g_ref[...]).astype(BF16)

    h = h_ref[...]

    def proj(w_ref):
        return jnp.dot(h, w_ref[...], preferred_element_type=F32)

    glu = proj(wav_ref) * _sigmoid(proj(wag_ref))
    glu_ref[...] = glu.reshape(sb, tm, cb)
    cbh_ref[...] = (proj(wbc_ref) * proj(wbh_ref)).reshape(sb, tm, cb)
    bb_ref[...] = proj(wbb_ref).reshape(sb, tm, cb)


def _inproj(x, norm_g, w_in, layer, sb, tm, cb):
    s, t, d = x.shape
    d_a = w_in.shape[-1] // 5
    nj = d_a // cb
    grid = (s // sb, t // tm, nj)

    def wspec(group):
        return pl.BlockSpec((None, d, cb), lambda i, k, j: (layer, 0, group * nj + j))

    act_spec = pl.BlockSpec((sb, tm, cb), lambda i, k, j: (i, k, j))
    out_sds = jax.ShapeDtypeStruct((s, t, d_a), F32)
    return pl.pallas_call(
        _inproj_kernel,
        grid=grid,
        in_specs=[
            pl.BlockSpec((sb, tm, d), lambda i, k, j: (i, k, 0)),
            pl.BlockSpec((None, 1, d), lambda i, k, j: (layer, 0, 0)),
            wspec(0), wspec(1), wspec(2), wspec(3), wspec(4),
        ],
        out_specs=[act_spec, act_spec, act_spec],
        out_shape=[out_sds, out_sds, out_sds],
        scratch_shapes=[pltpu.VMEM((sb * tm, d), BF16)],
        compiler_params=_params(3),
        name="inproj",
    )(x, norm_g, w_in, w_in, w_in, w_in, w_in)


def _mixout_kernel(x_ref, glu_ref, cbh_ref, bb_ref, hista_ref, histb_ref,
                   wa_ref, ba_ref, lng_ref, lnb_ref, wb_ref, wout_ref,
                   o_ref, pada_ref, padb_ref, mix_ref):
    sb, tm, d = x_ref.shape
    d_a = glu_ref.shape[-1]
    d_b = cbh_ref.shape[-1]
    nca = d_a // V7X_LANES
    ncb = d_b // V7X_LANES
    a0 = CONV_A_PAD - (wa_ref.shape[0] - 1)
    b0 = CONV3_PAD - (wb_ref.shape[0] - 1)

    @pl.when(pl.program_id(1) == 0)
    def _():
        for c in range(nca):
            pada_ref[:, c, 0:CONV_A_PAD, :] = hista_ref[:, :, _lanes(c)]
        for c in range(ncb):
            padb_ref[:, c, 0:CONV3_PAD, :] = histb_ref[:, :, _lanes(c)]

    for c in range(nca):
        pada_ref[:, c, CONV_A_PAD:, :] = glu_ref[:, :, _lanes(c)]
    for c in range(ncb):
        padb_ref[:, c, CONV3_PAD:, :] = cbh_ref[:, :, _lanes(c)]

    rc = min(CONV_ROWS, tm)

    def chunk(s_idx, r0):
        y = jnp.concatenate(
            [_conv_taps(pada_ref, (s_idx, c), r0 + a0, rc, wa_ref, c) for c in range(nca)],
            axis=-1) + ba_ref[...]
        mu = jnp.mean(y, axis=-1, keepdims=True)
        yc = y - mu
        var = jnp.mean(yc * yc, axis=-1, keepdims=True)
        yn = yc * lax.rsqrt(var + EPS) * lng_ref[...] + lnb_ref[...]
        row0 = pl.multiple_of(s_idx * tm + r0, rc)
        mix_ref[pl.ds(row0, rc), 0:d_a] = (yn * _sigmoid(yn)).astype(BF16)

        u = jnp.concatenate(
            [_conv_taps(padb_ref, (s_idx, c), r0 + b0, rc, wb_ref, c) for c in range(ncb)],
            axis=-1)
        yb = bb_ref[s_idx, pl.ds(r0, rc), :] * u
        mix_ref[pl.ds(row0, rc), d_a:] = yb.astype(BF16)

    for s_idx in range(sb):
        if tm == rc:
            chunk(s_idx, 0)
        else:
            def body(i, carry, s_idx=s_idx):
                chunk(s_idx, pl.multiple_of(i * rc, rc))
                return carry
            lax.fori_loop(0, tm // rc, body, 0)

    y = jnp.dot(mix_ref[...], wout_ref[...], preferred_element_type=F32)
    o_ref[...] = x_ref[...] + y.reshape(sb, tm, d)

    pada_ref[:, :, 0:CONV_A_PAD, :] = pada_ref[:, :, tm:tm + CONV_A_PAD, :]
    padb_ref[:, :, 0:CONV3_PAD, :] = padb_ref[:, :, tm:tm + CONV3_PAD, :]


def _mixout(x, glu, cbh, bb, hist_a, hist_b, conv_a_w, conv_a_b, ln_g, ln_b,
            conv_b_w, w_out, layer, sb, tm):
    s, t, d = x.shape
    d_a = glu.shape[-1]
    d_b = cbh.shape[-1]
    ka = conv_a_w.shape[1]
    kb = conv_b_w.shape[1]
    grid = (s // sb, t // tm)
    tile = lambda c: pl.BlockSpec((sb, tm, c), lambda i, k: (i, k, 0))
    vec = lambda c: _resident((None, 1, c), lambda i, k: (layer, 0, 0))
    return pl.pallas_call(
        _mixout_kernel,
        grid=grid,
        in_specs=[
            tile(d), tile(d_a), tile(d_b), tile(d_b),
            pl.BlockSpec((sb, CONV_A_PAD, d_a), lambda i, k: (i, 0, 0)),
            pl.BlockSpec((sb, CONV3_PAD, d_b), lambda i, k: (i, 0, 0)),
            _resident((None, ka, d_a), lambda i, k: (layer, 0, 0)),
            vec(d_a), vec(d_a), vec(d_a),
            _resident((None, kb, d_b), lambda i, k: (layer, 0, 0)),
            _resident((None, d_a + d_b, d), lambda i, k: (layer, 0, 0)),
        ],
        out_specs=tile(d),
        out_shape=jax.ShapeDtypeStruct((s, t, d), F32),
        scratch_shapes=[
            pltpu.VMEM((sb, d_a // V7X_LANES, CONV_A_PAD + tm, V7X_LANES), F32),
            pltpu.VMEM((sb, d_b // V7X_LANES, CONV3_PAD + tm, V7X_LANES), F32),
            pltpu.VMEM((sb * tm, d_a + d_b), BF16),
        ],
        compiler_params=_params(2),
        name="mixout",
    )(x, glu, cbh, bb, hist_a, hist_b, conv_a_w, conv_a_b, ln_g, ln_b, conv_b_w, w_out)


def _ffn_kernel(x_ref, g_ref, wg_ref, wv_ref, cwg_ref, cwv_ref, histg_ref, histv_ref,
                wd_ref, o_ref, newg_ref, newv_ref, h_ref, pad0_ref, pad1_ref,
                carry_ref, act0_ref, act1_ref):
    sb, tm, d = x_ref.shape
    fb = wg_ref.shape[-1]
    nc = fb // V7X_LANES
    kf = cwg_ref.shape[0]
    nf = carry_ref.shape[0]
    rows = sb * tm
    p0 = CONV3_PAD - (kf - 1)
    t_idx = pl.program_id(1)
    j = pl.program_id(2)
    pads = (pad0_ref, pad1_ref)
    acts = (act0_ref, act1_ref)

    def stage_a(pad_ref):
        @pl.when(t_idx == 0)
        def _():
            for c in range(nc):
                pad_ref[:, c, 0:CONV3_PAD, :] = histg_ref[:, :, _lanes(c)]
                pad_ref[:, nc + c, 0:CONV3_PAD, :] = histv_ref[:, :, _lanes(c)]

        @pl.when(t_idx > 0)
        def _():
            pad_ref[:, :, 0:CONV3_PAD, :] = carry_ref[j]

        h = h_ref[...]
        ug = jnp.dot(h, wg_ref[...], preferred_element_type=F32).reshape(sb, tm, fb)
        uv = jnp.dot(h, wv_ref[...], preferred_element_type=F32).reshape(sb, tm, fb)
        for c in range(nc):
            pad_ref[:, c, CONV3_PAD:, :] = ug[:, :, _lanes(c)]
            pad_ref[:, nc + c, CONV3_PAD:, :] = uv[:, :, _lanes(c)]
        carry_ref[j] = pad_ref[:, :, tm:tm + CONV3_PAD, :]
        for c in range(nc):
            newg_ref[:, :, _lanes(c)] = pad_ref[:, c, tm + p0:tm + CONV3_PAD, :]
            newv_ref[:, :, _lanes(c)] = pad_ref[:, nc + c, tm + p0:tm + CONV3_PAD, :]

    def stage_b(pad_ref, act_ref):
        rc = min(2 * CONV_ROWS, tm)
        for s_idx in range(sb):
            for r0 in range(0, tm, rc):
                for c in range(nc):
                    cg = _conv_taps(pad_ref, (s_idx, c), r0 + p0, rc, cwg_ref, c)
                    cv = _conv_taps(pad_ref, (s_idx, nc + c), r0 + p0, rc, cwv_ref, c)
                    row0 = s_idx * tm + r0
                    act_ref[row0:row0 + rc, _lanes(c)] = (cg * _sigmoid(cg) * cv).astype(BF16)

    def stage_c(act_ref):
        y = jnp.dot(act_ref[...], wd_ref[...], preferred_element_type=F32)
        o_ref[...] += y.reshape(sb, tm, d)

    @pl.when(j == 0)
    def _():
        x = x_ref[...]
        h_ref[...] = _rms_norm_rows(x.reshape(rows, d), g_ref[...]).astype(BF16)
        o_ref[...] = x
        stage_a(pads[0])

    @pl.when(j == 1)
    def _():
        stage_a(pads[1])
        stage_b(pads[0], acts[0])

    for parity in (0, 1):
        @pl.when((j >= 2) & (j < nf) & (j % 2 == parity))
        def _(parity=parity):
            stage_a(pads[parity])
            stage_b(pads[1 - parity], acts[1 - parity])
            stage_c(acts[parity])

    @pl.when(j == nf)
    def _():
        stage_b(pads[(nf - 1) % 2], acts[(nf - 1) % 2])
        stage_c(acts[nf % 2])

    @pl.when(j == nf + 1)
    def _():
        stage_c(acts[(nf - 1) % 2])


def _ffn(x, norm_g, w_up, conv_w, hist, w_down, layer, sb, tm, fb):
    s, t, d = x.shape
    d_ff = w_down.shape[1]
    kf = conv_w.shape[1]
    nf = d_ff // fb
    assert nf >= 2
    grid = (s // sb, t // tm, nf + 2)
    cur = lambda j: jnp.minimum(j, nf - 1)
    prv = lambda j: jnp.clip(j - 1, 0, nf - 1)
    pp = lambda j: jnp.clip(j - 2, 0, nf - 1)
    tile = pl.BlockSpec((sb, tm, d), lambda i, k, j: (i, k, 0))
    new_spec = pl.BlockSpec((sb, None, kf - 1, fb), lambda i, k, j: (i, k, 0, cur(j)))
    new_sds = jax.ShapeDtypeStruct((s, t // tm, kf - 1, d_ff), F32)
    pad = pltpu.VMEM((sb, 2 * fb // V7X_LANES, CONV3_PAD + tm, V7X_LANES), F32)
    act = pltpu.VMEM((sb * tm, fb), BF16)
    x_new, new_g, new_v = pl.pallas_call(
        _ffn_kernel,
        grid=grid,
        in_specs=[
            tile,
            pl.BlockSpec((None, 1, d), lambda i, k, j: (layer, 0, 0)),
            pl.BlockSpec((None, d, fb), lambda i, k, j: (layer, 0, cur(j))),
            pl.BlockSpec((None, d, fb), lambda i, k, j: (layer, 0, nf + cur(j))),
            pl.BlockSpec((None, kf, fb), lambda i, k, j: (layer, 0, prv(j))),
            pl.BlockSpec((None, kf, fb), lambda i, k, j: (layer, 0, nf + prv(j))),
            pl.BlockSpec((sb, CONV3_PAD, fb), lambda i, k, j: (i, 0, cur(j))),
            pl.BlockSpec((sb, CONV3_PAD, fb), lambda i, k, j: (i, 0, nf + cur(j))),
            pl.BlockSpec((None, fb, d), lambda i, k, j: (layer, pp(j), 0)),
        ],
        out_specs=[tile, new_spec, new_spec],
        out_shape=[jax.ShapeDtypeStruct((s, t, d), F32), new_sds, new_sds],
        scratch_shapes=[
            pltpu.VMEM((sb * tm, d), BF16),
            pad, pad,
            pltpu.VMEM((nf, sb, 2 * fb // V7X_LANES, CONV3_PAD, V7X_LANES), F32),
            act, act,
        ],
        compiler_params=_params(3),
        name="convffn",
    )(x, norm_g, w_up, w_up, conv_w, conv_w, hist, hist, w_down)
    return x_new, jnp.concatenate([new_g[:, -1], new_v[:, -1]], axis=-1)


def _final_norm_kernel(x_ref, g_ref, o_ref):
    sb, tm, d = x_ref.shape
    x = x_ref[...].reshape(sb * tm, d)
    o_ref[...] = _rms_norm_rows(x, g_ref[...]).reshape(sb, tm, d)


def _final_norm(x, g, sb, tm):
    s, t, d = x.shape
    tile = pl.BlockSpec((sb, tm, d), lambda i, k: (i, k, 0))
    return pl.pallas_call(
        _final_norm_kernel,
        grid=(s // sb, t // tm),
        in_specs=[tile, _resident((1, d), lambda i, k: (0, 0))],
        out_specs=tile,
        out_shape=jax.ShapeDtypeStruct((s, t, d), F32),
        compiler_params=_params(2),
        name="finalnorm",
    )(x, g)


def _tiling(s, t):
    tm = min(t, 512)
    sb = s if s * t <= 512 else 1
    return sb, tm


def _front_pad(state, rows):
    return jnp.pad(state, ((0, 0), (0, 0), (rows - state.shape[2], 0), (0, 0)))


def _trunk(x, state_a, state_b, state_f, p):
    s, t, d = x.shape
    sb, tm = _tiling(s, t)
    depth = p["w_in"].shape[0]
    ka1, kb1 = state_a.shape[2], state_b.shape[2]
    hist_a = _front_pad(state_a, CONV_A_PAD)
    hist_b = _front_pad(state_b, CONV3_PAD)
    hist_f = _front_pad(state_f, CONV3_PAD)
    new_a, new_b, new_f = [], [], []
    for layer in range(depth):
        glu, cbh, bb = _inproj(x, p["norm1_g"], p["w_in"], layer, sb, tm, cb=256)
        new_a.append(glu[:, t - ka1:, :])
        new_b.append(cbh[:, t - kb1:, :])
        x = _mixout(x, glu, cbh, bb, hist_a[layer], hist_b[layer],
                    p["conv_a_w"], p["conv_a_b"], p["ln_a_g"], p["ln_a_b"],
                    p["conv_b_w"], p["w_out"], layer, sb, tm)
        x, nf = _ffn(x, p["norm2_g"], p["w_up"], p["conv_ffn_w"], hist_f[layer],
                     p["w_down"], layer, sb, tm, fb=512)
        new_f.append(nf)
    y = _final_norm(x, p["final_g"], sb, tm)
    return y, jnp.stack(new_a), jnp.stack(new_b), jnp.stack(new_f)


def kernel(x_prompt, x_sample, state_conv_a, state_conv_b, state_ffn, norm1_g, w_in,
           conv_a_w, conv_a_b, ln_a_g, ln_a_b, conv_b_w, w_out, norm2_g, w_up,
           conv_ffn_w, w_down, final_g):
    depth = w_in.shape[0]
    vec = lambda a: a.reshape(depth, 1, a.shape[-1])
    p = dict(
        norm1_g=vec(norm1_g), w_in=w_in.astype(BF16),
        conv_a_w=conv_a_w, conv_a_b=vec(conv_a_b), ln_a_g=vec(ln_a_g), ln_a_b=vec(ln_a_b),
        conv_b_w=conv_b_w, w_out=w_out.astype(BF16),
        norm2_g=vec(norm2_g), w_up=w_up.astype(BF16), conv_ffn_w=conv_ffn_w,
        w_down=w_down.astype(BF16), final_g=final_g.reshape(1, -1),
    )
    bp = x_prompt.shape[0]
    no_history = lambda st: jnp.zeros((depth, bp) + st.shape[2:], st.dtype)
    yp, pa, pb, pf = _trunk(x_prompt, no_history(state_conv_a), no_history(state_conv_b),
                            no_history(state_ffn), p)
    ys, sa, sbb, sf = _trunk(x_sample, state_conv_a, state_conv_b, state_ffn, p)
    return yp, ys, pa, pb, pf, sa, sbb, sf
```

```python
import functools

import jax
import jax.numpy as jnp
from jax import lax
from jax.experimental import pallas as pl
from jax.experimental.pallas import tpu as pltpu

F32 = jnp.float32
BF16 = jnp.bfloat16
EPS = 1e-6

V7X_SUBLANES = 8
V7X_LANES = 128
V7X_VMEM_LIMIT_BYTES = 56 * 1024 * 1024

CONV_A_PAD = 32
CONV3_PAD = V7X_SUBLANES
CONV_ROWS = 32
NORM_ROWS = 16
NORM_UNROLL = 4
CONV_BLOCK_ROWS = 128
MIXOUT_CHUNKS = 4


def _sigmoid(x):
    return 1.0 / (1.0 + jnp.exp(-x))


def _rms_norm_rows(x, g):
    y = x * lax.rsqrt(jnp.mean(x * x, axis=-1, keepdims=True) + EPS)
    return y * g


def _lanes(c):
    return slice(c * V7X_LANES, (c + 1) * V7X_LANES)


def _params(n_axes):
    return pltpu.CompilerParams(
        dimension_semantics=("arbitrary",) * n_axes,
        vmem_limit_bytes=V7X_VMEM_LIMIT_BYTES,
    )


def _resident(block_shape, index_map):
    return pl.BlockSpec(block_shape, index_map, pipeline_mode=pl.Buffered(1))


def _conv_taps(pad_ref, lead, first_row, rows, w_ref, lane0):
    sub = V7X_SUBLANES
    groups = rows // sub
    taps = w_ref.shape[0]
    acc = [None] * groups
    for r in range(min(sub, taps)):
        ws = [w_ref[k, :, pl.ds(lane0, V7X_LANES)] for k in range(r, taps, sub)]
        for m in range(groups + len(ws) - 1):
            win = pad_ref[lead + (pl.ds(first_row + sub * m + r, sub), slice(None))]
            for i, w in enumerate(ws):
                g = m - i
                if 0 <= g < groups:
                    term = win * w
                    acc[g] = term if acc[g] is None else acc[g] + term
    return jnp.concatenate(acc, axis=0)


def _inproj_kernel(x_ref, g_ref, wav_ref, wag_ref, wbb_ref, wbc_ref, wbh_ref,
                   glu_ref, cbh_ref, bb_ref, h_ref):
    sb, tm, d = x_ref.shape
    cb = glu_ref.shape[-1]
    rows = sb * tm

    @pl.when(pl.program_id(2) == 0)
    def _():
        x = x_ref[...].reshape(rows, d)
        h_ref[...] = _rms_norm_rows(x, g_ref[...]).astype(BF16)

    h = h_ref[...]

    def proj(w_ref):
        return jnp.dot(h, w_ref[...], preferred_element_type=F32)

    glu = proj(wav_ref) * _sigmoid(proj(wag_ref))
    glu_ref[...] = glu.reshape(sb, tm, cb)
    cbh_ref[...] = (proj(wbc_ref) * proj(wbh_ref)).reshape(sb, tm, cb)
    bb_ref[...] = proj(wbb_ref).reshape(sb, tm, cb)


def _inproj(x, norm_g, w_in, layer, sb, tm, cb):
    s, t, d = x.shape
    d_a = w_in.shape[-1] // 5
    nj = d_a // cb
    grid = (s // sb, t // tm, nj)

    def wspec(group):
        return pl.BlockSpec((None, d, cb), lambda i, k, j: (layer, 0, group * nj + j))

    act_spec = pl.BlockSpec((sb, tm, cb), lambda i, k, j: (i, k, j))
    out_sds = jax.ShapeDtypeStruct((s, t, d_a), F32)
    return pl.pallas_call(
        _inproj_kernel,
        grid=grid,
        in_specs=[
            pl.BlockSpec((sb, tm, d), lambda i, k, j: (i, k, 0)),
            pl.BlockSpec((None, 1, d), lambda i, k, j: (layer, 0, 0)),
            wspec(0), wspec(1), wspec(2), wspec(3), wspec(4),
        ],
        out_specs=[act_spec, act_spec, act_spec],
        out_shape=[out_sds, out_sds, out_sds],
        scratch_shapes=[pltpu.VMEM((sb * tm, d), BF16)],
        compiler_params=_params(3),
        name="inproj",
    )(x, norm_g, w_in, w_in, w_in, w_in, w_in)


def _mixout_kernel(x_ref, glu_ref, cbh_ref, bb_ref, hista_ref, histb_ref,
                   wa_ref, ba_ref, lng_ref, lnb_ref, wb_ref, wout_ref,
                   o_ref, pada_ref, padb_ref, conva_ref, mix0_ref, mix1_ref,
                   *, tiles_per_seq, n_chunks):
    sb, tm, d = x_ref.shape
    d_a = glu_ref.shape[-1]
    d_b = cbh_ref.shape[-1]
    nca = d_a // V7X_LANES
    ncb = d_b // V7X_LANES
    a0 = CONV_A_PAD - (wa_ref.shape[0] - 1)
    b0 = CONV3_PAD - (wb_ref.shape[0] - 1)
    rc = min(NORM_ROWS, tm)
    cr = min(CONV_BLOCK_ROWS, tm)
    dn = d // n_chunks
    q = pl.program_id(0)

    @pl.when(q == 0)
    def _():
        mix1_ref[...] = jnp.zeros(mix1_ref.shape, mix1_ref.dtype)

    @pl.when(q % tiles_per_seq == 0)
    def _():
        for c in range(nca):
            pada_ref[:, c, 0:CONV_A_PAD, :] = hista_ref[:, :, _lanes(c)]
        for c in range(ncb):
            padb_ref[:, c, 0:CONV3_PAD, :] = histb_ref[:, :, _lanes(c)]

    for c in range(nca):
        pada_ref[:, c, CONV_A_PAD:, :] = glu_ref[:, :, _lanes(c)]
    for c in range(ncb):
        padb_ref[:, c, CONV3_PAD:, :] = cbh_ref[:, :, _lanes(c)]

    def step(mix_cur, mix_prev):
        def chunk(i, carry):
            col0 = pl.multiple_of(i * dn, dn)
            y = jnp.dot(mix_prev[...], wout_ref[:, pl.ds(col0, dn)], preferred_element_type=F32)
            o_ref[:, :, pl.ds(col0, dn)] = x_ref[:, :, pl.ds(col0, dn)] + y.reshape(sb, tm, dn)
            for cc in range(nca // n_chunks):
                c = i * (nca // n_chunks) + cc
                lane0 = pl.multiple_of(c * V7X_LANES, V7X_LANES)
                for s_idx in range(sb):
                    for r0 in range(0, tm, cr):
                        row0 = s_idx * tm + r0
                        conva_ref[row0:row0 + cr, pl.ds(lane0, V7X_LANES)] = _conv_taps(
                            pada_ref, (s_idx, c), r0 + a0, cr, wa_ref, lane0)
            for cc in range(ncb // n_chunks):
                c = i * (ncb // n_chunks) + cc
                lane0 = pl.multiple_of(c * V7X_LANES, V7X_LANES)
                for s_idx in range(sb):
                    for r0 in range(0, tm, cr):
                        row0 = s_idx * tm + r0
                        u = _conv_taps(padb_ref, (s_idx, c), r0 + b0, cr, wb_ref, lane0)
                        yb = bb_ref[s_idx, r0:r0 + cr, pl.ds(lane0, V7X_LANES)] * u
                        mix_cur[row0:row0 + cr, pl.ds(d_a + lane0, V7X_LANES)] = yb.astype(BF16)
            return carry

        lax.fori_loop(0, n_chunks, chunk, 0)

        def norm(i, carry):
            row0 = pl.multiple_of(i * rc, rc)
            y = conva_ref[pl.ds(row0, rc), :] + ba_ref[...]
            mu = jnp.mean(y, axis=-1, keepdims=True)
            yc = y - mu
            var = jnp.mean(yc * yc, axis=-1, keepdims=True)
            yn = yc * lax.rsqrt(var + EPS) * lng_ref[...] + lnb_ref[...]
            mix_cur[pl.ds(row0, rc), 0:d_a] = (yn * _sigmoid(yn)).astype(BF16)
            return carry

        lax.fori_loop(0, sb * tm // rc, norm, 0, unroll=NORM_UNROLL)

    for parity, (cur, prev) in enumerate(((mix0_ref, mix1_ref), (mix1_ref, mix0_ref))):
        @pl.when(q % 2 == parity)
        def _(cur=cur, prev=prev):
            step(cur, prev)

    pada_ref[:, :, 0:CONV_A_PAD, :] = pada_ref[:, :, tm:tm + CONV_A_PAD, :]
    padb_ref[:, :, 0:CONV3_PAD, :] = padb_ref[:, :, tm:tm + CONV3_PAD, :]


def _mixout(x, glu, cbh, bb, hist_a, hist_b, conv_a_w, conv_a_b, ln_g, ln_b,
            conv_b_w, w_out, layer, sb, tm):
    s, t, d = x.shape
    d_a = glu.shape[-1]
    d_b = cbh.shape[-1]
    ka = conv_a_w.shape[1]
    kb = conv_b_w.shape[1]
    tps = t // tm
    n_tiles = (s // sb) * tps
    cur = lambda q: jnp.minimum(q, n_tiles - 1)
    prv = lambda q: jnp.maximum(q - 1, 0)
    conv_tile = lambda c: pl.BlockSpec((sb, tm, c), lambda q: (cur(q) // tps, cur(q) % tps, 0))
    out_tile = pl.BlockSpec((sb, tm, d), lambda q: (prv(q) // tps, prv(q) % tps, 0))
    hist = lambda rows, c: pl.BlockSpec((sb, rows, c), lambda q: (cur(q) // tps, 0, 0))
    vec = lambda c: _resident((None, 1, c), lambda q: (layer, 0, 0))
    return pl.pallas_call(
        functools.partial(_mixout_kernel, tiles_per_seq=tps, n_chunks=MIXOUT_CHUNKS),
        grid=(n_tiles + 1,),
        in_specs=[
            out_tile, conv_tile(d_a), conv_tile(d_b), conv_tile(d_b),
            hist(CONV_A_PAD, d_a), hist(CONV3_PAD, d_b),
            _resident((None, ka, V7X_SUBLANES, d_a), lambda q: (layer, 0, 0, 0)),
            vec(d_a), vec(d_a), vec(d_a),
            _resident((None, kb, V7X_SUBLANES, d_b), lambda q: (layer, 0, 0, 0)),
            _resident((None, d_a + d_b, d), lambda q: (layer, 0, 0)),
        ],
        out_specs=out_tile,
        out_shape=jax.ShapeDtypeStruct((s, t, d), F32),
        scratch_shapes=[
            pltpu.VMEM((sb, d_a // V7X_LANES, CONV_A_PAD + tm, V7X_LANES), F32),
            pltpu.VMEM((sb, d_b // V7X_LANES, CONV3_PAD + tm, V7X_LANES), F32),
            pltpu.VMEM((sb * tm, d_a), F32),
            pltpu.VMEM((sb * tm, d_a + d_b), BF16),
            pltpu.VMEM((sb * tm, d_a + d_b), BF16),
        ],
        compiler_params=_params(1),
        name="mixout",
    )(x, glu, cbh, bb, hist_a, hist_b, conv_a_w, conv_a_b, ln_g, ln_b, conv_b_w, w_out)


def _ffn_kernel(x_ref, g_ref, wg_ref, wv_ref, cwg_ref, cwv_ref, histg_ref, histv_ref,
                wd_ref, fg_ref, o_ref, newg_ref, newv_ref, h_ref, pad_ref, carry_ref, act_ref,
                *, final_norm):
    sb, tm, d = x_ref.shape
    fb = wg_ref.shape[-1]
    nc = fb // V7X_LANES
    kf = cwg_ref.shape[0]
    rows = sb * tm
    p0 = CONV3_PAD - (kf - 1)
    t_idx = pl.program_id(1)
    j = pl.program_id(2)

    @pl.when(j == 0)
    def _():
        x = x_ref[...].reshape(rows, d)
        h_ref[...] = _rms_norm_rows(x, g_ref[...]).astype(BF16)

    @pl.when(t_idx == 0)
    def _():
        for c in range(nc):
            pad_ref[:, c, 0:CONV3_PAD, :] = histg_ref[:, :, _lanes(c)]
            pad_ref[:, nc + c, 0:CONV3_PAD, :] = histv_ref[:, :, _lanes(c)]

    @pl.when(t_idx > 0)
    def _():
        pad_ref[:, :, 0:CONV3_PAD, :] = carry_ref[j]

    h = h_ref[...]
    ug = jnp.dot(h, wg_ref[...], preferred_element_type=F32).reshape(sb, tm, fb)
    uv = jnp.dot(h, wv_ref[...], preferred_element_type=F32).reshape(sb, tm, fb)
    for c in range(nc):
        pad_ref[:, c, CONV3_PAD:, :] = ug[:, :, _lanes(c)]
        pad_ref[:, nc + c, CONV3_PAD:, :] = uv[:, :, _lanes(c)]

    carry_ref[j] = pad_ref[:, :, tm:tm + CONV3_PAD, :]
    for c in range(nc):
        newg_ref[:, :, _lanes(c)] = pad_ref[:, c, tm + p0:tm + CONV3_PAD, :]
        newv_ref[:, :, _lanes(c)] = pad_ref[:, nc + c, tm + p0:tm + CONV3_PAD, :]

    rc = min(2 * CONV_ROWS, tm)
    for s_idx in range(sb):
        for r0 in range(0, tm, rc):
            for c in range(nc):
                cg = _conv_taps(pad_ref, (s_idx, c), r0 + p0, rc, cwg_ref, c * V7X_LANES)
                cv = _conv_taps(pad_ref, (s_idx, nc + c), r0 + p0, rc, cwv_ref, c * V7X_LANES)
                row0 = s_idx * tm + r0
                act_ref[row0:row0 + rc, _lanes(c)] = (cg * _sigmoid(cg) * cv).astype(BF16)

    y = jnp.dot(act_ref[...], wd_ref[...], preferred_element_type=F32).reshape(sb, tm, d)

    @pl.when(j == 0)
    def _():
        o_ref[...] = x_ref[...] + y

    @pl.when(j > 0)
    def _():
        o_ref[...] += y

    if final_norm:
        @pl.when(j == pl.num_programs(2) - 1)
        def _():
            o = o_ref[...].reshape(rows, d)
            o_ref[...] = _rms_norm_rows(o, fg_ref[...]).reshape(sb, tm, d)


def _ffn(x, norm_g, w_up, conv_w, hist, w_down, final_g, layer, sb, tm, fb, final_norm):
    s, t, d = x.shape
    d_ff = w_down.shape[1]
    kf = conv_w.shape[1]
    nf = d_ff // fb
    grid = (s // sb, t // tm, nf)
    tile = pl.BlockSpec((sb, tm, d), lambda i, k, j: (i, k, 0))
    new_spec = pl.BlockSpec((sb, None, kf - 1, fb), lambda i, k, j: (i, k, 0, j))
    new_sds = jax.ShapeDtypeStruct((s, t // tm, kf - 1, d_ff), F32)
    x_new, new_g, new_v = pl.pallas_call(
        functools.partial(_ffn_kernel, final_norm=final_norm),
        grid=grid,
        in_specs=[
            tile,
            pl.BlockSpec((None, 1, d), lambda i, k, j: (layer, 0, 0)),
            pl.BlockSpec((None, d, fb), lambda i, k, j: (layer, 0, j)),
            pl.BlockSpec((None, d, fb), lambda i, k, j: (layer, 0, nf + j)),
            pl.BlockSpec((None, kf, V7X_SUBLANES, fb), lambda i, k, j: (layer, 0, 0, j)),
            pl.BlockSpec((None, kf, V7X_SUBLANES, fb), lambda i, k, j: (layer, 0, 0, nf + j)),
            pl.BlockSpec((sb, CONV3_PAD, fb), lambda i, k, j: (i, 0, j)),
            pl.BlockSpec((sb, CONV3_PAD, fb), lambda i, k, j: (i, 0, nf + j)),
            pl.BlockSpec((None, fb, d), lambda i, k, j: (layer, j, 0)),
            pl.BlockSpec((1, d), lambda i, k, j: (0, 0)),
        ],
        out_specs=[tile, new_spec, new_spec],
        out_shape=[jax.ShapeDtypeStruct((s, t, d), F32), new_sds, new_sds],
        scratch_shapes=[
            pltpu.VMEM((sb * tm, d), BF16),
            pltpu.VMEM((sb, 2 * fb // V7X_LANES, CONV3_PAD + tm, V7X_LANES), F32),
            pltpu.VMEM((nf, sb, 2 * fb // V7X_LANES, CONV3_PAD, V7X_LANES), F32),
            pltpu.VMEM((sb * tm, fb), BF16),
        ],
        compiler_params=_params(3),
        name="convffn",
    )(x, norm_g, w_up, w_up, conv_w, conv_w, hist, hist, w_down, final_g)
    return x_new, jnp.concatenate([new_g[:, -1], new_v[:, -1]], axis=-1)


def _tiling(s, t):
    tm = min(t, 512)
    sb = s if s * t <= 512 else 1
    return sb, tm


def _front_pad(state, rows):
    return jnp.pad(state, ((0, 0), (0, 0), (rows - state.shape[2], 0), (0, 0)))


def _trunk(x, state_a, state_b, state_f, p):
    s, t, d = x.shape
    sb, tm = _tiling(s, t)
    tm_in = min(t, 2 * tm) if sb == 1 else tm
    depth = p["w_in"].shape[0]
    ka1, kb1 = state_a.shape[2], state_b.shape[2]
    hist_a = _front_pad(state_a, CONV_A_PAD)
    hist_b = _front_pad(state_b, CONV3_PAD)
    hist_f = _front_pad(state_f, CONV3_PAD)
    new_a, new_b, new_f = [], [], []
    for layer in range(depth):
        glu, cbh, bb = _inproj(x, p["norm1_g"], p["w_in"], layer, sb, tm_in, cb=256)
        new_a.append(glu[:, t - ka1:, :])
        new_b.append(cbh[:, t - kb1:, :])
        x = _mixout(x, glu, cbh, bb, hist_a[layer], hist_b[layer],
                    p["conv_a_w"], p["conv_a_b"], p["ln_a_g"], p["ln_a_b"],
                    p["conv_b_w"], p["w_out"], layer, sb, tm)
        x, nf = _ffn(x, p["norm2_g"], p["w_up"], p["conv_ffn_w"], hist_f[layer],
                     p["w_down"], p["final_g"], layer, sb, tm, fb=512,
                     final_norm=(layer == depth - 1))
        new_f.append(nf)
    return x, jnp.stack(new_a), jnp.stack(new_b), jnp.stack(new_f)


def kernel(x_prompt, x_sample, state_conv_a, state_conv_b, state_ffn, norm1_g, w_in,
           conv_a_w, conv_a_b, ln_a_g, ln_a_b, conv_b_w, w_out, norm2_g, w_up,
           conv_ffn_w, w_down, final_g):
    depth = w_in.shape[0]
    vec = lambda a: a.reshape(depth, 1, a.shape[-1])
    taps = lambda w: jnp.broadcast_to(w[:, :, None, :], w.shape[:2] + (V7X_SUBLANES, w.shape[2]))
    p = dict(
        norm1_g=vec(norm1_g), w_in=w_in.astype(BF16),
        conv_a_w=taps(conv_a_w), conv_a_b=vec(conv_a_b), ln_a_g=vec(ln_a_g), ln_a_b=vec(ln_a_b),
        conv_b_w=taps(conv_b_w), w_out=w_out.astype(BF16),
        norm2_g=vec(norm2_g), w_up=w_up.astype(BF16), conv_ffn_w=taps(conv_ffn_w),
        w_down=w_down.astype(BF16), final_g=final_g.reshape(1, -1),
    )
    bp = x_prompt.shape[0]
    no_history = lambda st: jnp.zeros((depth, bp) + st.shape[2:], st.dtype)
    yp, pa, pb, pf = _trunk(x_prompt, no_history(state_conv_a), no_history(state_conv_b),
                            no_history(state_ffn), p)
    ys, sa, sbb, sf = _trunk(x_sample, state_conv_a, state_conv_b, state_ffn, p)
    return yp, ys, pa, pb, pf, sa, sbb, sf
```

```python
import functools

import jax
import jax.numpy as jnp
from jax import lax
from jax.experimental import pallas as pl
from jax.experimental.pallas import tpu as pltpu

F32 = jnp.float32
BF16 = jnp.bfloat16
EPS = 1e-6

V7X_SUBLANES = 8
V7X_LANES = 128
V7X_VMEM_LIMIT_BYTES = 56 * 1024 * 1024

CONV_A_PAD = 32
CONV3_PAD = V7X_SUBLANES
CONV_ROWS = 32
NORM_ROWS = 16
NORM_UNROLL = 4
CONV_BLOCK_ROWS = 128
MIXOUT_CHUNKS = 2


def _sigmoid(x):
    return 1.0 / (1.0 + jnp.exp(-x))


def _rms_norm_rows(x, g):
    y = x * lax.rsqrt(jnp.mean(x * x, axis=-1, keepdims=True) + EPS)
    return y * g


def _lanes(c):
    return slice(c * V7X_LANES, (c + 1) * V7X_LANES)


def _params(n_axes):
    return pltpu.CompilerParams(
        dimension_semantics=("arbitrary",) * n_axes,
        vmem_limit_bytes=V7X_VMEM_LIMIT_BYTES,
    )


def _resident(block_shape, index_map):
    return pl.BlockSpec(block_shape, index_map, pipeline_mode=pl.Buffered(1))


def _conv_taps(pad_ref, lead, first_row, rows, w_ref, lane0):
    sub = V7X_SUBLANES
    groups = rows // sub
    taps = w_ref.shape[0]
    acc = [None] * groups
    for r in range(min(sub, taps)):
        ws = [w_ref[k, :, pl.ds(lane0, V7X_LANES)] for k in range(r, taps, sub)]
        for m in range(groups + len(ws) - 1):
            win = pad_ref[lead + (pl.ds(first_row + sub * m + r, sub), slice(None))]
            for i, w in enumerate(ws):
                g = m - i
                if 0 <= g < groups:
                    term = win * w
                    acc[g] = term if acc[g] is None else acc[g] + term
    return jnp.concatenate(acc, axis=0)


def _inproj_kernel(x_ref, g_ref, wav_ref, wag_ref, wbb_ref, wbc_ref, wbh_ref,
                   glu_ref, cbh_ref, bb_ref, h_ref):
    sb, tm, d = x_ref.shape
    cb = glu_ref.shape[-1]
    rows = sb * tm

    @pl.when(pl.program_id(2) == 0)
    def _():
        x = x_ref[...].reshape(rows, d)
        h_ref[...] = _rms_norm_rows(x, g_ref[...]).astype(BF16)

    h = h_ref[...]

    def proj(w_ref):
        return jnp.dot(h, w_ref[...], preferred_element_type=F32)

    glu = proj(wav_ref) * _sigmoid(proj(wag_ref))
    glu_ref[...] = glu.reshape(sb, tm, cb)
    cbh_ref[...] = (proj(wbc_ref) * proj(wbh_ref)).reshape(sb, tm, cb)
    bb_ref[...] = proj(wbb_ref).reshape(sb, tm, cb)


def _inproj(x, norm_g, w_in, layer, sb, tm, cb):
    s, t, d = x.shape
    d_a = w_in.shape[-1] // 5
    nj = d_a // cb
    grid = (s // sb, t // tm, nj)

    def wspec(group):
        return pl.BlockSpec((None, d, cb), lambda i, k, j: (layer, 0, group * nj + j))

    act_spec = pl.BlockSpec((sb, tm, cb), lambda i, k, j: (i, k, j))
    out_sds = jax.ShapeDtypeStruct((s, t, d_a), F32)
    return pl.pallas_call(
        _inproj_kernel,
        grid=grid,
        in_specs=[
            pl.BlockSpec((sb, tm, d), lambda i, k, j: (i, k, 0)),
            pl.BlockSpec((None, 1, d), lambda i, k, j: (layer, 0, 0)),
            wspec(0), wspec(1), wspec(2), wspec(3), wspec(4),
        ],
        out_specs=[act_spec, act_spec, act_spec],
        out_shape=[out_sds, out_sds, out_sds],
        scratch_shapes=[pltpu.VMEM((sb * tm, d), BF16)],
        compiler_params=_params(3),
        name="inproj",
    )(x, norm_g, w_in, w_in, w_in, w_in, w_in)


def _mixout_kernel(x_ref, glu_ref, cbh_ref, bb_ref, hista_ref, histb_ref,
                   wa_ref, ba_ref, lng_ref, lnb_ref, wb_ref, wout_ref,
                   o_ref, pada_ref, padb_ref, conva_ref, mix0_ref, mix1_ref,
                   *, tiles_per_seq, n_chunks):
    sb, tm, d = x_ref.shape
    d_a = glu_ref.shape[-1]
    d_b = cbh_ref.shape[-1]
    nca = d_a // V7X_LANES
    ncb = d_b // V7X_LANES
    a0 = CONV_A_PAD - (wa_ref.shape[0] - 1)
    b0 = CONV3_PAD - (wb_ref.shape[0] - 1)
    rc = min(NORM_ROWS, tm)
    cr = min(CONV_BLOCK_ROWS, tm)
    dn = d // n_chunks
    q = pl.program_id(0)

    @pl.when(q == 0)
    def _():
        mix1_ref[...] = jnp.zeros(mix1_ref.shape, mix1_ref.dtype)

    @pl.when(q % tiles_per_seq == 0)
    def _():
        for c in range(nca):
            pada_ref[:, c, 0:CONV_A_PAD, :] = hista_ref[:, :, _lanes(c)]
        for c in range(ncb):
            padb_ref[:, c, 0:CONV3_PAD, :] = histb_ref[:, :, _lanes(c)]

    for c in range(nca):
        pada_ref[:, c, CONV_A_PAD:, :] = glu_ref[:, :, _lanes(c)]
    for c in range(ncb):
        padb_ref[:, c, CONV3_PAD:, :] = cbh_ref[:, :, _lanes(c)]

    def step(mix_cur, mix_prev):
        def chunk(i, carry):
            col0 = pl.multiple_of(i * dn, dn)
            y = jnp.dot(mix_prev[...], wout_ref[:, pl.ds(col0, dn)], preferred_element_type=F32)
            o_ref[:, :, pl.ds(col0, dn)] = x_ref[:, :, pl.ds(col0, dn)] + y.reshape(sb, tm, dn)
            for cc in range(nca // n_chunks):
                c = i * (nca // n_chunks) + cc
                lane0 = pl.multiple_of(c * V7X_LANES, V7X_LANES)
                for s_idx in range(sb):
                    for r0 in range(0, tm, cr):
                        row0 = s_idx * tm + r0
                        conva_ref[row0:row0 + cr, pl.ds(lane0, V7X_LANES)] = _conv_taps(
                            pada_ref, (s_idx, c), r0 + a0, cr, wa_ref, lane0)
            for cc in range(ncb // n_chunks):
                c = i * (ncb // n_chunks) + cc
                lane0 = pl.multiple_of(c * V7X_LANES, V7X_LANES)
                for s_idx in range(sb):
                    for r0 in range(0, tm, cr):
                        row0 = s_idx * tm + r0
                        u = _conv_taps(padb_ref, (s_idx, c), r0 + b0, cr, wb_ref, lane0)
                        yb = bb_ref[s_idx, r0:r0 + cr, pl.ds(lane0, V7X_LANES)] * u
                        mix_cur[row0:row0 + cr, pl.ds(d_a + lane0, V7X_LANES)] = yb.astype(BF16)
            return carry

        lax.fori_loop(0, n_chunks, chunk, 0)

        def norm(i, carry):
            row0 = pl.multiple_of(i * rc, rc)
            y = conva_ref[pl.ds(row0, rc), :] + ba_ref[...]
            mu = jnp.mean(y, axis=-1, keepdims=True)
            yc = y - mu
            var = jnp.mean(yc * yc, axis=-1, keepdims=True)
            yn = yc * lax.rsqrt(var + EPS) * lng_ref[...] + lnb_ref[...]
            mix_cur[pl.ds(row0, rc), 0:d_a] = (yn * _sigmoid(yn)).astype(BF16)
            return carry

        lax.fori_loop(0, sb * tm // rc, norm, 0, unroll=NORM_UNROLL)

    for parity, (cur, prev) in enumerate(((mix0_ref, mix1_ref), (mix1_ref, mix0_ref))):
        @pl.when(q % 2 == parity)
        def _(cur=cur, prev=prev):
            step(cur, prev)

    pada_ref[:, :, 0:CONV_A_PAD, :] = pada_ref[:, :, tm:tm + CONV_A_PAD, :]
    padb_ref[:, :, 0:CONV3_PAD, :] = padb_ref[:, :, tm:tm + CONV3_PAD, :]


def _mixout(x, glu, cbh, bb, hist_a, hist_b, conv_a_w, conv_a_b, ln_g, ln_b,
            conv_b_w, w_out, layer, sb, tm):
    s, t, d = x.shape
    d_a = glu.shape[-1]
    d_b = cbh.shape[-1]
    ka = conv_a_w.shape[1]
    kb = conv_b_w.shape[1]
    tps = t // tm
    n_tiles = (s // sb) * tps
    cur = lambda q: jnp.minimum(q, n_tiles - 1)
    prv = lambda q: jnp.maximum(q - 1, 0)
    conv_tile = lambda c: pl.BlockSpec((sb, tm, c), lambda q: (cur(q) // tps, cur(q) % tps, 0))
    out_tile = pl.BlockSpec((sb, tm, d), lambda q: (prv(q) // tps, prv(q) % tps, 0))
    hist = lambda rows, c: pl.BlockSpec((sb, rows, c), lambda q: (cur(q) // tps, 0, 0))
    vec = lambda c: _resident((None, 1, c), lambda q: (layer, 0, 0))
    return pl.pallas_call(
        functools.partial(_mixout_kernel, tiles_per_seq=tps, n_chunks=MIXOUT_CHUNKS),
        grid=(n_tiles + 1,),
        in_specs=[
            out_tile, conv_tile(d_a), conv_tile(d_b), conv_tile(d_b),
            hist(CONV_A_PAD, d_a), hist(CONV3_PAD, d_b),
            _resident((None, ka, V7X_SUBLANES, d_a), lambda q: (layer, 0, 0, 0)),
            vec(d_a), vec(d_a), vec(d_a),
            _resident((None, kb, V7X_SUBLANES, d_b), lambda q: (layer, 0, 0, 0)),
            _resident((None, d_a + d_b, d), lambda q: (layer, 0, 0)),
        ],
        out_specs=out_tile,
        out_shape=jax.ShapeDtypeStruct((s, t, d), F32),
        scratch_shapes=[
            pltpu.VMEM((sb, d_a // V7X_LANES, CONV_A_PAD + tm, V7X_LANES), F32),
            pltpu.VMEM((sb, d_b // V7X_LANES, CONV3_PAD + tm, V7X_LANES), F32),
            pltpu.VMEM((sb * tm, d_a), F32),
            pltpu.VMEM((sb * tm, d_a + d_b), BF16),
            pltpu.VMEM((sb * tm, d_a + d_b), BF16),
        ],
        compiler_params=_params(1),
        name="mixout",
    )(x, glu, cbh, bb, hist_a, hist_b, conv_a_w, conv_a_b, ln_g, ln_b, conv_b_w, w_out)


def _ffn_kernel(x_ref, g_ref, wg_ref, wv_ref, cwg_ref, cwv_ref, histg_ref, histv_ref,
                wd_ref, fg_ref, o_ref, newg_ref, newv_ref, h_ref, pad0_ref, pad1_ref,
                carry_ref, act_ref, *, n_blocks, final_norm):
    sb, tm, d = x_ref.shape
    fb = wg_ref.shape[-1]
    nc = fb // V7X_LANES
    kf = cwg_ref.shape[0]
    nf = carry_ref.shape[0]
    rows = sb * tm
    p0 = CONV3_PAD - (kf - 1)
    q = pl.program_id(1)
    qa = jnp.minimum(q, n_blocks - 1)
    qb = jnp.maximum(q - 1, 0)
    ta, ja = qa // nf, qa % nf
    jb = qb % nf
    pads = (pad0_ref, pad1_ref)

    @pl.when((q < n_blocks) & (ja == 0))
    def _():
        x = x_ref[...].reshape(rows, d)
        h_ref[...] = _rms_norm_rows(x, g_ref[...]).astype(BF16)

    @pl.when((q > 0) & (jb == 0))
    def _():
        o_ref[...] = x_ref[...]

    def stage_a(pad_ref):
        @pl.when(ta == 0)
        def _():
            for c in range(nc):
                pad_ref[:, c, 0:CONV3_PAD, :] = histg_ref[:, :, _lanes(c)]
                pad_ref[:, nc + c, 0:CONV3_PAD, :] = histv_ref[:, :, _lanes(c)]

        @pl.when(ta > 0)
        def _():
            pad_ref[:, :, 0:CONV3_PAD, :] = carry_ref[ja]

        h = h_ref[...]
        ug = jnp.dot(h, wg_ref[...], preferred_element_type=F32).reshape(sb, tm, fb)
        uv = jnp.dot(h, wv_ref[...], preferred_element_type=F32).reshape(sb, tm, fb)
        for c in range(nc):
            pad_ref[:, c, CONV3_PAD:, :] = ug[:, :, _lanes(c)]
            pad_ref[:, nc + c, CONV3_PAD:, :] = uv[:, :, _lanes(c)]
        carry_ref[ja] = pad_ref[:, :, tm:tm + CONV3_PAD, :]
        for c in range(nc):
            newg_ref[:, :, _lanes(c)] = pad_ref[:, c, tm + p0:tm + CONV3_PAD, :]
            newv_ref[:, :, _lanes(c)] = pad_ref[:, nc + c, tm + p0:tm + CONV3_PAD, :]

    def stage_b(pad_ref):
        rc = min(2 * CONV_ROWS, tm)
        for s_idx in range(sb):
            for r0 in range(0, tm, rc):
                for c in range(nc):
                    cg = _conv_taps(pad_ref, (s_idx, c), r0 + p0, rc, cwg_ref, c * V7X_LANES)
                    cv = _conv_taps(pad_ref, (s_idx, nc + c), r0 + p0, rc, cwv_ref, c * V7X_LANES)
                    row0 = s_idx * tm + r0
                    act_ref[row0:row0 + rc, _lanes(c)] = (cg * _sigmoid(cg) * cv).astype(BF16)
        y = jnp.dot(act_ref[...], wd_ref[...], preferred_element_type=F32)
        o_ref[...] += y.reshape(sb, tm, d)

        if final_norm:
            @pl.when(jb == nf - 1)
            def _():
                o = o_ref[...].reshape(rows, d)
                o_ref[...] = _rms_norm_rows(o, fg_ref[...]).reshape(sb, tm, d)

    @pl.when(q == 0)
    def _():
        stage_a(pads[0])

    for parity in (0, 1):
        @pl.when((q > 0) & (q < n_blocks) & (q % 2 == parity))
        def _(parity=parity):
            stage_a(pads[parity])
            stage_b(pads[1 - parity])

    @pl.when(q == n_blocks)
    def _():
        stage_b(pads[(n_blocks - 1) % 2])


def _ffn(x, norm_g, w_up, conv_w, hist, w_down, final_g, layer, sb, tm, fb, final_norm):
    s, t, d = x.shape
    d_ff = w_down.shape[1]
    kf = conv_w.shape[1]
    nf = d_ff // fb
    assert nf >= 2
    n_blocks = (t // tm) * nf
    qa = lambda q: jnp.minimum(q, n_blocks - 1)
    qb = lambda q: jnp.maximum(q - 1, 0)
    x_tile = pl.BlockSpec((sb, tm, d), lambda i, q: (i, qa(q) // nf, 0))
    o_tile = pl.BlockSpec((sb, tm, d), lambda i, q: (i, qb(q) // nf, 0))
    new_spec = pl.BlockSpec((sb, None, kf - 1, fb), lambda i, q: (i, qa(q) // nf, 0, qa(q) % nf))
    new_sds = jax.ShapeDtypeStruct((s, t // tm, kf - 1, d_ff), F32)
    pad = pltpu.VMEM((sb, 2 * fb // V7X_LANES, CONV3_PAD + tm, V7X_LANES), F32)
    taps = lambda half: pl.BlockSpec((None, kf, V7X_SUBLANES, fb),
                                     lambda i, q: (layer, 0, 0, half * nf + qb(q) % nf))
    x_new, new_g, new_v = pl.pallas_call(
        functools.partial(_ffn_kernel, n_blocks=n_blocks, final_norm=final_norm),
        grid=(s // sb, n_blocks + 1),
        in_specs=[
            x_tile,
            pl.BlockSpec((None, 1, d), lambda i, q: (layer, 0, 0)),
            pl.BlockSpec((None, d, fb), lambda i, q: (layer, 0, qa(q) % nf)),
            pl.BlockSpec((None, d, fb), lambda i, q: (layer, 0, nf + qa(q) % nf)),
            taps(0), taps(1),
            pl.BlockSpec((sb, CONV3_PAD, fb), lambda i, q: (i, 0, qa(q) % nf)),
            pl.BlockSpec((sb, CONV3_PAD, fb), lambda i, q: (i, 0, nf + qa(q) % nf)),
            pl.BlockSpec((None, fb, d), lambda i, q: (layer, qb(q) % nf, 0)),
            pl.BlockSpec((1, d), lambda i, q: (0, 0)),
        ],
        out_specs=[o_tile, new_spec, new_spec],
        out_shape=[jax.ShapeDtypeStruct((s, t, d), F32), new_sds, new_sds],
        scratch_shapes=[
            pltpu.VMEM((sb * tm, d), BF16),
            pad, pad,
            pltpu.VMEM((nf, sb, 2 * fb // V7X_LANES, CONV3_PAD, V7X_LANES), F32),
            pltpu.VMEM((sb * tm, fb), BF16),
        ],
        compiler_params=_params(2),
        name="convffn",
    )(x, norm_g, w_up, w_up, conv_w, conv_w, hist, hist, w_down, final_g)
    return x_new, jnp.concatenate([new_g[:, -1], new_v[:, -1]], axis=-1)


def _tiling(s, t):
    tm = min(t, 512)
    sb = s if s * t <= 512 else 1
    return sb, tm


def _front_pad(state, rows):
    return jnp.pad(state, ((0, 0), (0, 0), (rows - state.shape[2], 0), (0, 0)))


def _trunk(x, state_a, state_b, state_f, p):
    s, t, d = x.shape
    sb, tm = _tiling(s, t)
    tm_in = min(t, 2 * tm) if sb == 1 else tm
    depth = p["w_in"].shape[0]
    ka1, kb1 = state_a.shape[2], state_b.shape[2]
    hist_a = _front_pad(state_a, CONV_A_PAD)
    hist_b = _front_pad(state_b, CONV3_PAD)
    hist_f = _front_pad(state_f, CONV3_PAD)
    new_a, new_b, new_f = [], [], []
    for layer in range(depth):
        glu, cbh, bb = _inproj(x, p["norm1_g"], p["w_in"], layer, sb, tm_in, cb=256)
        new_a.append(glu[:, t - ka1:, :])
        new_b.append(cbh[:, t - kb1:, :])
        x = _mixout(x, glu, cbh, bb, hist_a[layer], hist_b[layer],
                    p["conv_a_w"], p["conv_a_b"], p["ln_a_g"], p["ln_a_b"],
                    p["conv_b_w"], p["w_out"], layer, sb, tm)
        x, nf = _ffn(x, p["norm2_g"], p["w_up"], p["conv_ffn_w"], hist_f[layer],
                     p["w_down"], p["final_g"], layer, sb, tm, fb=512,
                     final_norm=(layer == depth - 1))
        new_f.append(nf)
    return x, jnp.stack(new_a), jnp.stack(new_b), jnp.stack(new_f)


def kernel(x_prompt, x_sample, state_conv_a, state_conv_b, state_ffn, norm1_g, w_in,
           conv_a_w, conv_a_b, ln_a_g, ln_a_b, conv_b_w, w_out, norm2_g, w_up,
           conv_ffn_w, w_down, final_g):
    depth = w_in.shape[0]
    vec = lambda a: a.reshape(depth, 1, a.shape[-1])
    taps = lambda w: jnp.broadcast_to(w[:, :, None, :], w.shape[:2] + (V7X_SUBLANES, w.shape[2]))
    p = dict(
        norm1_g=vec(norm1_g), w_in=w_in.astype(BF16),
        conv_a_w=taps(conv_a_w), conv_a_b=vec(conv_a_b), ln_a_g=vec(ln_a_g), ln_a_b=vec(ln_a_b),
        conv_b_w=taps(conv_b_w), w_out=w_out.astype(BF16),
        norm2_g=vec(norm2_g), w_up=w_up.astype(BF16), conv_ffn_w=taps(conv_ffn_w),
        w_down=w_down.astype(BF16), final_g=final_g.reshape(1, -1),
    )
    bp = x_prompt.shape[0]
    no_history = lambda st: jnp.zeros((depth, bp) + st.shape[2:], st.dtype)
    yp, pa, pb, pf = _trunk(x_prompt, no_history(state_conv_a), no_history(state_conv_b),
                            no_history(state_ffn), p)
    ys, sa, sbb, sf = _trunk(x_sample, state_conv_a, state_conv_b, state_ffn, p)
    return yp, ys, pa, pb, pf, sa, sbb, sf
```

```python
import functools

import jax
import jax.numpy as jnp
from jax import lax
from jax.experimental import pallas as pl
from jax.experimental.pallas import tpu as pltpu

F32 = jnp.float32
BF16 = jnp.bfloat16
EPS = 1e-6

V7X_SUBLANES = 8
V7X_LANES = 128
V7X_VMEM_LIMIT_BYTES = 56 * 1024 * 1024

CONV_A_PAD = 32
CONV3_PAD = V7X_SUBLANES
CONV_ROWS = 32
NORM_ROWS = 16
NORM_UNROLL = 4
CONV_BLOCK_ROWS = 128
MIXOUT_CHUNKS = 2
INPROJ_COLS = 256
FFN_COLS = 512


def _sigmoid(x):
    return 1.0 / (1.0 + jnp.exp(-x))


def _rms_norm_rows(x, g):
    y = x * lax.rsqrt(jnp.mean(x * x, axis=-1, keepdims=True) + EPS)
    return y * g


def _lanes(c):
    return slice(c * V7X_LANES, (c + 1) * V7X_LANES)


def _params(n_axes):
    return pltpu.CompilerParams(
        dimension_semantics=("arbitrary",) * n_axes,
        vmem_limit_bytes=V7X_VMEM_LIMIT_BYTES,
    )


def _resident(block_shape, index_map):
    return pl.BlockSpec(block_shape, index_map, pipeline_mode=pl.Buffered(1))


def _conv_taps(pad_ref, lead, first_row, rows, w_ref, lane0):
    sub = V7X_SUBLANES
    groups = rows // sub
    taps = w_ref.shape[0]
    acc = [None] * groups
    for r in range(min(sub, taps)):
        ws = [w_ref[k, :, pl.ds(lane0, V7X_LANES)] for k in range(r, taps, sub)]
        for m in range(groups + len(ws) - 1):
            win = pad_ref[lead + (pl.ds(first_row + sub * m + r, sub), slice(None))]
            for i, w in enumerate(ws):
                g = m - i
                if 0 <= g < groups:
                    term = win * w
                    acc[g] = term if acc[g] is None else acc[g] + term
    return jnp.concatenate(acc, axis=0)


def _inproj_kernel(x_ref, g_ref, wav_ref, wag_ref, wbb_ref, wbc_ref, wbh_ref,
                   glu_ref, cbh_ref, bb_ref, h_ref):
    sb, tm, d = x_ref.shape
    cb = glu_ref.shape[-1]
    rows = sb * tm

    @pl.when(pl.program_id(2) == 0)
    def _():
        x = x_ref[...].reshape(rows, d)
        h_ref[...] = _rms_norm_rows(x, g_ref[...]).astype(BF16)

    h = h_ref[...]

    def proj(w_ref):
        return jnp.dot(h, w_ref[...], preferred_element_type=F32)

    glu = proj(wav_ref) * _sigmoid(proj(wag_ref))
    glu_ref[...] = glu.reshape(sb, tm, cb)
    cbh_ref[...] = (proj(wbc_ref) * proj(wbh_ref)).reshape(sb, tm, cb)
    bb_ref[...] = proj(wbb_ref).reshape(sb, tm, cb)


def _inproj(x, norm_g, w_in, layer, sb, tm):
    s, t, d = x.shape
    cb = w_in.shape[-1]
    nj = w_in.shape[1] // 5
    d_a = nj * cb
    grid = (s // sb, t // tm, nj)

    def wspec(group):
        return pl.BlockSpec((None, None, d, cb), lambda i, k, j: (layer, group * nj + j, 0, 0))

    act_spec = pl.BlockSpec((sb, None, tm, cb), lambda i, k, j: (i, j, k, 0))
    out_sds = jax.ShapeDtypeStruct((s, nj, t, cb), F32)
    return pl.pallas_call(
        _inproj_kernel,
        grid=grid,
        in_specs=[
            pl.BlockSpec((sb, tm, d), lambda i, k, j: (i, k, 0)),
            pl.BlockSpec((None, 1, d), lambda i, k, j: (layer, 0, 0)),
            wspec(0), wspec(1), wspec(2), wspec(3), wspec(4),
        ],
        out_specs=[act_spec, act_spec, act_spec],
        out_shape=[out_sds, out_sds, out_sds],
        scratch_shapes=[pltpu.VMEM((sb * tm, d), BF16)],
        compiler_params=_params(3),
        name="inproj",
    )(x, norm_g, w_in, w_in, w_in, w_in, w_in)


def _mixout_kernel(x_ref, glu_ref, cbh_ref, bb_ref, hista_ref, histb_ref,
                   wa_ref, ba_ref, lng_ref, lnb_ref, wb_ref, wout_ref,
                   o_ref, pada_ref, padb_ref, conva_ref, mix0_ref, mix1_ref,
                   *, tiles_per_seq, n_chunks):
    sb, tm, d = x_ref.shape
    cb = glu_ref.shape[-1]
    d_a = glu_ref.shape[1] * cb
    d_b = cbh_ref.shape[1] * cb
    spb = cb // V7X_LANES
    nca = d_a // V7X_LANES
    ncb = d_b // V7X_LANES
    a0 = CONV_A_PAD - (wa_ref.shape[0] - 1)
    b0 = CONV3_PAD - (wb_ref.shape[0] - 1)
    rc = min(NORM_ROWS, tm)
    cr = min(CONV_BLOCK_ROWS, tm)
    dn = d // n_chunks
    q = pl.program_id(0)

    @pl.when(q == 0)
    def _():
        mix1_ref[...] = jnp.zeros(mix1_ref.shape, mix1_ref.dtype)

    @pl.when(q % tiles_per_seq == 0)
    def _():
        for c in range(nca):
            pada_ref[:, c, 0:CONV_A_PAD, :] = hista_ref[:, :, _lanes(c)]
        for c in range(ncb):
            padb_ref[:, c, 0:CONV3_PAD, :] = histb_ref[:, :, _lanes(c)]

    for c in range(nca):
        pada_ref[:, c, CONV_A_PAD:, :] = glu_ref[:, c // spb, :, _lanes(c % spb)]
    for c in range(ncb):
        padb_ref[:, c, CONV3_PAD:, :] = cbh_ref[:, c // spb, :, _lanes(c % spb)]

    def step(mix_cur, mix_prev):
        def chunk(i, carry):
            col0 = pl.multiple_of(i * dn, dn)
            y = jnp.dot(mix_prev[...], wout_ref[:, pl.ds(col0, dn)], preferred_element_type=F32)
            o_ref[:, :, pl.ds(col0, dn)] = x_ref[:, :, pl.ds(col0, dn)] + y.reshape(sb, tm, dn)
            for cc in range(nca // n_chunks):
                c = i * (nca // n_chunks) + cc
                lane0 = pl.multiple_of(c * V7X_LANES, V7X_LANES)
                for s_idx in range(sb):
                    for r0 in range(0, tm, cr):
                        row0 = s_idx * tm + r0
                        conva_ref[row0:row0 + cr, pl.ds(lane0, V7X_LANES)] = _conv_taps(
                            pada_ref, (s_idx, c), r0 + a0, cr, wa_ref, lane0)
            for cc in range(ncb // n_chunks):
                c = i * (ncb // n_chunks) + cc
                lane0 = pl.multiple_of(c * V7X_LANES, V7X_LANES)
                for s_idx in range(sb):
                    for r0 in range(0, tm, cr):
                        row0 = s_idx * tm + r0
                        u = _conv_taps(padb_ref, (s_idx, c), r0 + b0, cr, wb_ref, lane0)
                        gate_lane0 = pl.multiple_of((c % spb) * V7X_LANES, V7X_LANES)
                        yb = bb_ref[s_idx, c // spb, r0:r0 + cr, pl.ds(gate_lane0, V7X_LANES)] * u
                        mix_cur[row0:row0 + cr, pl.ds(d_a + lane0, V7X_LANES)] = yb.astype(BF16)
            return carry

        lax.fori_loop(0, n_chunks, chunk, 0)

        def norm(i, carry):
            row0 = pl.multiple_of(i * rc, rc)
            y = conva_ref[pl.ds(row0, rc), :] + ba_ref[...]
            mu = jnp.mean(y, axis=-1, keepdims=True)
            yc = y - mu
            var = jnp.mean(yc * yc, axis=-1, keepdims=True)
            yn = yc * lax.rsqrt(var + EPS) * lng_ref[...] + lnb_ref[...]
            mix_cur[pl.ds(row0, rc), 0:d_a] = (yn * _sigmoid(yn)).astype(BF16)
            return carry

        lax.fori_loop(0, sb * tm // rc, norm, 0, unroll=NORM_UNROLL)

    for parity, (cur, prev) in enumerate(((mix0_ref, mix1_ref), (mix1_ref, mix0_ref))):
        @pl.when(q % 2 == parity)
        def _(cur=cur, prev=prev):
            step(cur, prev)

    pada_ref[:, :, 0:CONV_A_PAD, :] = pada_ref[:, :, tm:tm + CONV_A_PAD, :]
    padb_ref[:, :, 0:CONV3_PAD, :] = padb_ref[:, :, tm:tm + CONV3_PAD, :]


def _mixout(x, glu, cbh, bb, hist_a, hist_b, conv_a_w, conv_a_b, ln_g, ln_b,
            conv_b_w, w_out, layer, sb, tm):
    s, t, d = x.shape
    d_a = glu.shape[1] * glu.shape[3]
    d_b = cbh.shape[1] * cbh.shape[3]
    ka = conv_a_w.shape[1]
    kb = conv_b_w.shape[1]
    tps = t // tm
    n_tiles = (s // sb) * tps
    cur = lambda q: jnp.minimum(q, n_tiles - 1)
    prv = lambda q: jnp.maximum(q - 1, 0)
    conv_tile = lambda a: pl.BlockSpec((sb, a.shape[1], tm, a.shape[3]),
                                       lambda q: (cur(q) // tps, 0, cur(q) % tps, 0))
    out_tile = pl.BlockSpec((sb, tm, d), lambda q: (prv(q) // tps, prv(q) % tps, 0))
    hist = lambda rows, c: pl.BlockSpec((sb, rows, c), lambda q: (cur(q) // tps, 0, 0))
    vec = lambda c: _resident((None, 1, c), lambda q: (layer, 0, 0))
    return pl.pallas_call(
        functools.partial(_mixout_kernel, tiles_per_seq=tps, n_chunks=MIXOUT_CHUNKS),
        grid=(n_tiles + 1,),
        in_specs=[
            out_tile, conv_tile(glu), conv_tile(cbh), conv_tile(bb),
            hist(CONV_A_PAD, d_a), hist(CONV3_PAD, d_b),
            _resident((None, ka, V7X_SUBLANES, d_a), lambda q: (layer, 0, 0, 0)),
            vec(d_a), vec(d_a), vec(d_a),
            _resident((None, kb, V7X_SUBLANES, d_b), lambda q: (layer, 0, 0, 0)),
            _resident((None, d_a + d_b, d), lambda q: (layer, 0, 0)),
        ],
        out_specs=out_tile,
        out_shape=jax.ShapeDtypeStruct((s, t, d), F32),
        scratch_shapes=[
            pltpu.VMEM((sb, d_a // V7X_LANES, CONV_A_PAD + tm, V7X_LANES), F32),
            pltpu.VMEM((sb, d_b // V7X_LANES, CONV3_PAD + tm, V7X_LANES), F32),
            pltpu.VMEM((sb * tm, d_a), F32),
            pltpu.VMEM((sb * tm, d_a + d_b), BF16),
            pltpu.VMEM((sb * tm, d_a + d_b), BF16),
        ],
        compiler_params=_params(1),
        name="mixout",
    )(x, glu, cbh, bb, hist_a, hist_b, conv_a_w, conv_a_b, ln_g, ln_b, conv_b_w, w_out)


def _ffn_kernel(x_ref, g_ref, wg_ref, wv_ref, cwg_ref, cwv_ref, histg_ref, histv_ref,
                wd_ref, fg_ref, o_ref, newg_ref, newv_ref, h_ref, pad0_ref, pad1_ref,
                carry_ref, act_ref, *, n_blocks, final_norm):
    sb, tm, d = x_ref.shape
    fb = wg_ref.shape[-1]
    nc = fb // V7X_LANES
    kf = cwg_ref.shape[0]
    nf = carry_ref.shape[0]
    rows = sb * tm
    p0 = CONV3_PAD - (kf - 1)
    q = pl.program_id(1)
    qa = jnp.minimum(q, n_blocks - 1)
    qb = jnp.maximum(q - 1, 0)
    ta, ja = qa // nf, qa % nf
    jb = qb % nf
    pads = (pad0_ref, pad1_ref)

    @pl.when((q < n_blocks) & (ja == 0))
    def _():
        x = x_ref[...].reshape(rows, d)
        h_ref[...] = _rms_norm_rows(x, g_ref[...]).astype(BF16)

    @pl.when((q > 0) & (jb == 0))
    def _():
        o_ref[...] = x_ref[...]

    def stage_a(pad_ref):
        @pl.when(ta == 0)
        def _():
            for c in range(nc):
                pad_ref[:, c, 0:CONV3_PAD, :] = histg_ref[:, :, _lanes(c)]
                pad_ref[:, nc + c, 0:CONV3_PAD, :] = histv_ref[:, :, _lanes(c)]

        @pl.when(ta > 0)
        def _():
            pad_ref[:, :, 0:CONV3_PAD, :] = carry_ref[ja]

        h = h_ref[...]
        ug = jnp.dot(h, wg_ref[...], preferred_element_type=F32).reshape(sb, tm, fb)
        uv = jnp.dot(h, wv_ref[...], preferred_element_type=F32).reshape(sb, tm, fb)
        for c in range(nc):
            pad_ref[:, c, CONV3_PAD:, :] = ug[:, :, _lanes(c)]
            pad_ref[:, nc + c, CONV3_PAD:, :] = uv[:, :, _lanes(c)]
        carry_ref[ja] = pad_ref[:, :, tm:tm + CONV3_PAD, :]
        for c in range(nc):
            newg_ref[:, :, _lanes(c)] = pad_ref[:, c, tm + p0:tm + CONV3_PAD, :]
            newv_ref[:, :, _lanes(c)] = pad_ref[:, nc + c, tm + p0:tm + CONV3_PAD, :]

    def stage_b(pad_ref):
        rc = min(2 * CONV_ROWS, tm)
        for s_idx in range(sb):
            for r0 in range(0, tm, rc):
                for c in range(nc):
                    cg = _conv_taps(pad_ref, (s_idx, c), r0 + p0, rc, cwg_ref, c * V7X_LANES)
                    cv = _conv_taps(pad_ref, (s_idx, nc + c), r0 + p0, rc, cwv_ref, c * V7X_LANES)
                    row0 = s_idx * tm + r0
                    act_ref[row0:row0 + rc, _lanes(c)] = (cg * _sigmoid(cg) * cv).astype(BF16)
        y = jnp.dot(act_ref[...], wd_ref[...], preferred_element_type=F32)
        o_ref[...] += y.reshape(sb, tm, d)

        if final_norm:
            @pl.when(jb == nf - 1)
            def _():
                o = o_ref[...].reshape(rows, d)
                o_ref[...] = _rms_norm_rows(o, fg_ref[...]).reshape(sb, tm, d)

    @pl.when(q == 0)
    def _():
        stage_a(pads[0])

    for parity in (0, 1):
        @pl.when((q > 0) & (q < n_blocks) & (q % 2 == parity))
        def _(parity=parity):
            stage_a(pads[parity])
            stage_b(pads[1 - parity])

    @pl.when(q == n_blocks)
    def _():
        stage_b(pads[(n_blocks - 1) % 2])


def _ffn(x, norm_g, w_up, conv_w, hist, w_down, final_g, layer, sb, tm, final_norm):
    s, t, d = x.shape
    d_ff = w_down.shape[1]
    kf = conv_w.shape[1]
    fb = w_up.shape[-1]
    nf = d_ff // fb
    assert nf >= 2 and w_up.shape[1] == 2 * nf
    n_blocks = (t // tm) * nf
    qa = lambda q: jnp.minimum(q, n_blocks - 1)
    qb = lambda q: jnp.maximum(q - 1, 0)
    x_tile = pl.BlockSpec((sb, tm, d), lambda i, q: (i, qa(q) // nf, 0))
    o_tile = pl.BlockSpec((sb, tm, d), lambda i, q: (i, qb(q) // nf, 0))
    new_spec = pl.BlockSpec((sb, None, kf - 1, fb), lambda i, q: (i, qa(q) // nf, 0, qa(q) % nf))
    new_sds = jax.ShapeDtypeStruct((s, t // tm, kf - 1, d_ff), F32)
    pad = pltpu.VMEM((sb, 2 * fb // V7X_LANES, CONV3_PAD + tm, V7X_LANES), F32)
    taps = lambda half: pl.BlockSpec((None, kf, V7X_SUBLANES, fb),
                                     lambda i, q: (layer, 0, 0, half * nf + qb(q) % nf))
    x_new, new_g, new_v = pl.pallas_call(
        functools.partial(_ffn_kernel, n_blocks=n_blocks, final_norm=final_norm),
        grid=(s // sb, n_blocks + 1),
        in_specs=[
            x_tile,
            pl.BlockSpec((None, 1, d), lambda i, q: (layer, 0, 0)),
            pl.BlockSpec((None, None, d, fb), lambda i, q: (layer, qa(q) % nf, 0, 0)),
            pl.BlockSpec((None, None, d, fb), lambda i, q: (layer, nf + qa(q) % nf, 0, 0)),
            taps(0), taps(1),
            pl.BlockSpec((sb, CONV3_PAD, fb), lambda i, q: (i, 0, qa(q) % nf)),
            pl.BlockSpec((sb, CONV3_PAD, fb), lambda i, q: (i, 0, nf + qa(q) % nf)),
            pl.BlockSpec((None, fb, d), lambda i, q: (layer, qb(q) % nf, 0)),
            pl.BlockSpec((1, d), lambda i, q: (0, 0)),
        ],
        out_specs=[o_tile, new_spec, new_spec],
        out_shape=[jax.ShapeDtypeStruct((s, t, d), F32), new_sds, new_sds],
        scratch_shapes=[
            pltpu.VMEM((sb * tm, d), BF16),
            pad, pad,
            pltpu.VMEM((nf, sb, 2 * fb // V7X_LANES, CONV3_PAD, V7X_LANES), F32),
            pltpu.VMEM((sb * tm, fb), BF16),
        ],
        compiler_params=_params(2),
        name="convffn",
    )(x, norm_g, w_up, w_up, conv_w, conv_w, hist, hist, w_down, final_g)
    return x_new, jnp.concatenate([new_g[:, -1], new_v[:, -1]], axis=-1)


def _tiling(s, t):
    tm = min(t, 512)
    sb = s if s * t <= 512 else 1
    return sb, tm


def _front_pad(state, rows):
    return jnp.pad(state, ((0, 0), (0, 0), (rows - state.shape[2], 0), (0, 0)))


def _unblock(a):
    s, nb, rows, cb = a.shape
    return a.transpose(0, 2, 1, 3).reshape(s, rows, nb * cb)


def _trunk(x, state_a, state_b, state_f, p):
    s, t, d = x.shape
    sb, tm = _tiling(s, t)
    tm_in = min(t, 2 * tm) if sb == 1 else tm
    depth = p["w_in"].shape[0]
    ka1, kb1 = state_a.shape[2], state_b.shape[2]
    hist_a = _front_pad(state_a, CONV_A_PAD)
    hist_b = _front_pad(state_b, CONV3_PAD)
    hist_f = _front_pad(state_f, CONV3_PAD)
    new_a, new_b, new_f = [], [], []
    for layer in range(depth):
        glu, cbh, bb = _inproj(x, p["norm1_g"], p["w_in"], layer, sb, tm_in)
        new_a.append(_unblock(glu[:, :, t - ka1:, :]))
        new_b.append(_unblock(cbh[:, :, t - kb1:, :]))
        x = _mixout(x, glu, cbh, bb, hist_a[layer], hist_b[layer],
                    p["conv_a_w"], p["conv_a_b"], p["ln_a_g"], p["ln_a_b"],
                    p["conv_b_w"], p["w_out"], layer, sb, tm)
        x, nf = _ffn(x, p["norm2_g"], p["w_up"], p["conv_ffn_w"], hist_f[layer],
                     p["w_down"], p["final_g"], layer, sb, tm,
                     final_norm=(layer == depth - 1))
        new_f.append(nf)
    return x, jnp.stack(new_a), jnp.stack(new_b), jnp.stack(new_f)


def kernel(x_prompt, x_sample, state_conv_a, state_conv_b, state_ffn, norm1_g, w_in,
           conv_a_w, conv_a_b, ln_a_g, ln_a_b, conv_b_w, w_out, norm2_g, w_up,
           conv_ffn_w, w_down, final_g):
    depth = w_in.shape[0]
    vec = lambda a: a.reshape(depth, 1, a.shape[-1])
    taps = lambda w: jnp.broadcast_to(w[:, :, None, :], w.shape[:2] + (V7X_SUBLANES, w.shape[2]))

    def col_blocked(w, width):
        rows, cols = w.shape[1:]
        return w.astype(BF16).reshape(depth, rows, cols // width, width).transpose(0, 2, 1, 3)

    p = dict(
        norm1_g=vec(norm1_g), w_in=col_blocked(w_in, INPROJ_COLS),
        conv_a_w=taps(conv_a_w), conv_a_b=vec(conv_a_b), ln_a_g=vec(ln_a_g), ln_a_b=vec(ln_a_b),
        conv_b_w=taps(conv_b_w), w_out=w_out.astype(BF16),
        norm2_g=vec(norm2_g), w_up=col_blocked(w_up, FFN_COLS), conv_ffn_w=taps(conv_ffn_w),
        w_down=w_down.astype(BF16), final_g=final_g.reshape(1, -1),
    )
    bp = x_prompt.shape[0]
    no_history = lambda st: jnp.zeros((depth, bp) + st.shape[2:], st.dtype)
    yp, pa, pb, pf = _trunk(x_prompt, no_history(state_conv_a), no_history(state_conv_b),
                            no_history(state_ffn), p)
    ys, sa, sbb, sf = _trunk(x_sample, state_conv_a, state_conv_b, state_ffn, p)
    return yp, ys, pa, pb, pf, sa, sbb, sf
```

```python
import functools

import jax
import jax.numpy as jnp
from jax import lax
from jax.experimental import pallas as pl
from jax.experimental.pallas import tpu as pltpu

F32 = jnp.float32
BF16 = jnp.bfloat16
EPS = 1e-6

V7X_SUBLANES = 8
V7X_LANES = 128
V7X_VMEM_LIMIT_BYTES = 56 * 1024 * 1024

CONV_A_PAD = 32
CONV3_PAD = V7X_SUBLANES
CONV_ROWS = 32
NORM_ROWS = 16
NORM_UNROLL = 16
CONV_BLOCK_ROWS = 128
MIXOUT_CHUNKS = 2


def _sigmoid(x):
    return 1.0 / (1.0 + jnp.exp(-x))


def _rms_norm_rows(x, g):
    y = x * lax.rsqrt(jnp.mean(x * x, axis=-1, keepdims=True) + EPS)
    return y * g


def _lanes(c):
    return slice(c * V7X_LANES, (c + 1) * V7X_LANES)


def _params(n_axes):
    return pltpu.CompilerParams(
        dimension_semantics=("arbitrary",) * n_axes,
        vmem_limit_bytes=V7X_VMEM_LIMIT_BYTES,
    )


def _resident(block_shape, index_map):
    return pl.BlockSpec(block_shape, index_map, pipeline_mode=pl.Buffered(1))


def _conv_taps(pad_ref, lead, first_row, rows, w_ref, lane0):
    sub = V7X_SUBLANES
    groups = rows // sub
    taps = w_ref.shape[0]
    acc = [None] * groups
    for r in range(min(sub, taps)):
        ws = [w_ref[k, :, pl.ds(lane0, V7X_LANES)] for k in range(r, taps, sub)]
        for m in range(groups + len(ws) - 1):
            win = pad_ref[lead + (pl.ds(first_row + sub * m + r, sub), slice(None))]
            for i, w in enumerate(ws):
                g = m - i
                if 0 <= g < groups:
                    term = win * w
                    acc[g] = term if acc[g] is None else acc[g] + term
    return jnp.concatenate(acc, axis=0)


def _inproj_kernel(x_ref, g_ref, wav_ref, wag_ref, wbb_ref, wbc_ref, wbh_ref,
                   glu_ref, cbh_ref, bb_ref, h_ref):
    sb, tm, d = x_ref.shape
    cb = glu_ref.shape[-1]
    rows = sb * tm

    @pl.when(pl.program_id(2) == 0)
    def _():
        x = x_ref[...].reshape(rows, d)
        h_ref[...] = _rms_norm_rows(x, g_ref[...]).astype(BF16)

    h = h_ref[...]

    def proj(w_ref):
        return jnp.dot(h, w_ref[...], preferred_element_type=F32)

    glu = proj(wav_ref) * _sigmoid(proj(wag_ref))
    glu_ref[...] = glu.reshape(sb, tm, cb)
    cbh_ref[...] = (proj(wbc_ref) * proj(wbh_ref)).reshape(sb, tm, cb)
    bb_ref[...] = proj(wbb_ref).reshape(sb, tm, cb)


def _inproj(x, norm_g, w_in, layer, sb, tm, cb):
    s, t, d = x.shape
    d_a = w_in.shape[-1] // 5
    nj = d_a // cb
    grid = (s // sb, t // tm, nj)

    def wspec(group):
        return pl.BlockSpec((None, d, cb), lambda i, k, j: (layer, 0, group * nj + j))

    act_spec = pl.BlockSpec((sb, tm, cb), lambda i, k, j: (i, k, j))
    out_sds = jax.ShapeDtypeStruct((s, t, d_a), F32)
    return pl.pallas_call(
        _inproj_kernel,
        grid=grid,
        in_specs=[
            pl.BlockSpec((sb, tm, d), lambda i, k, j: (i, k, 0)),
            pl.BlockSpec((None, 1, d), lambda i, k, j: (layer, 0, 0)),
            wspec(0), wspec(1), wspec(2), wspec(3), wspec(4),
        ],
        out_specs=[act_spec, act_spec, act_spec],
        out_shape=[out_sds, out_sds, out_sds],
        scratch_shapes=[pltpu.VMEM((sb * tm, d), BF16)],
        compiler_params=_params(3),
        name="inproj",
    )(x, norm_g, w_in, w_in, w_in, w_in, w_in)


def _mixout_kernel(x_ref, glu_ref, cbh_ref, bb_ref, hista_ref, histb_ref,
                   wa_ref, ba_ref, lng_ref, lnb_ref, wb_ref, wout_ref,
                   o_ref, pada_ref, padb_ref, conva_ref, mix0_ref, mix1_ref,
                   *, tiles_per_seq, n_chunks):
    sb, tm, d = x_ref.shape
    d_a = glu_ref.shape[-1]
    d_b = cbh_ref.shape[-1]
    nca = d_a // V7X_LANES
    ncb = d_b // V7X_LANES
    a0 = CONV_A_PAD - (wa_ref.shape[0] - 1)
    b0 = CONV3_PAD - (wb_ref.shape[0] - 1)
    rc = min(NORM_ROWS, tm)
    cr = min(CONV_BLOCK_ROWS, tm)
    dn = d // n_chunks
    q = pl.program_id(0)

    @pl.when(q == 0)
    def _():
        mix1_ref[...] = jnp.zeros(mix1_ref.shape, mix1_ref.dtype)

    @pl.when(q % tiles_per_seq == 0)
    def _():
        for c in range(nca):
            pada_ref[:, c, 0:CONV_A_PAD, :] = hista_ref[:, :, _lanes(c)]
        for c in range(ncb):
            padb_ref[:, c, 0:CONV3_PAD, :] = histb_ref[:, :, _lanes(c)]

    for c in range(nca):
        pada_ref[:, c, CONV_A_PAD:, :] = glu_ref[:, :, _lanes(c)]
    for c in range(ncb):
        padb_ref[:, c, CONV3_PAD:, :] = cbh_ref[:, :, _lanes(c)]

    def step(mix_cur, mix_prev):
        def chunk(i, carry):
            col0 = pl.multiple_of(i * dn, dn)
            y = jnp.dot(mix_prev[...], wout_ref[:, pl.ds(col0, dn)], preferred_element_type=F32)
            o_ref[:, :, pl.ds(col0, dn)] = x_ref[:, :, pl.ds(col0, dn)] + y.reshape(sb, tm, dn)
            for cc in range(nca // n_chunks):
                c = i * (nca // n_chunks) + cc
                lane0 = pl.multiple_of(c * V7X_LANES, V7X_LANES)
                for s_idx in range(sb):
                    for r0 in range(0, tm, cr):
                        row0 = s_idx * tm + r0
                        conva_ref[row0:row0 + cr, pl.ds(lane0, V7X_LANES)] = _conv_taps(
                            pada_ref, (s_idx, c), r0 + a0, cr, wa_ref, lane0)
            for cc in range(ncb // n_chunks):
                c = i * (ncb // n_chunks) + cc
                lane0 = pl.multiple_of(c * V7X_LANES, V7X_LANES)
                for s_idx in range(sb):
                    for r0 in range(0, tm, cr):
                        row0 = s_idx * tm + r0
                        u = _conv_taps(padb_ref, (s_idx, c), r0 + b0, cr, wb_ref, lane0)
                        yb = bb_ref[s_idx, r0:r0 + cr, pl.ds(lane0, V7X_LANES)] * u
                        mix_cur[row0:row0 + cr, pl.ds(d_a + lane0, V7X_LANES)] = yb.astype(BF16)
            return carry

        lax.fori_loop(0, n_chunks, chunk, 0)

        def norm(i, carry):
            row0 = pl.multiple_of(i * rc, rc)
            y = conva_ref[pl.ds(row0, rc), :] + ba_ref[...]
            mu = jnp.mean(y, axis=-1, keepdims=True)
            yc = y - mu
            var = jnp.mean(yc * yc, axis=-1, keepdims=True)
            yn = yc * lax.rsqrt(var + EPS) * lng_ref[...] + lnb_ref[...]
            mix_cur[pl.ds(row0, rc), 0:d_a] = (yn * _sigmoid(yn)).astype(BF16)
            return carry

        lax.fori_loop(0, sb * tm // rc, norm, 0, unroll=NORM_UNROLL)

    for parity, (cur, prev) in enumerate(((mix0_ref, mix1_ref), (mix1_ref, mix0_ref))):
        @pl.when(q % 2 == parity)
        def _(cur=cur, prev=prev):
            step(cur, prev)

    pada_ref[:, :, 0:CONV_A_PAD, :] = pada_ref[:, :, tm:tm + CONV_A_PAD, :]
    padb_ref[:, :, 0:CONV3_PAD, :] = padb_ref[:, :, tm:tm + CONV3_PAD, :]


def _mixout(x, glu, cbh, bb, hist_a, hist_b, conv_a_w, conv_a_b, ln_g, ln_b,
            conv_b_w, w_out, layer, sb, tm):
    s, t, d = x.shape
    d_a = glu.shape[-1]
    d_b = cbh.shape[-1]
    ka = conv_a_w.shape[1]
    kb = conv_b_w.shape[1]
    tps = t // tm
    n_tiles = (s // sb) * tps
    cur = lambda q: jnp.minimum(q, n_tiles - 1)
    prv = lambda q: jnp.maximum(q - 1, 0)
    conv_tile = lambda c: pl.BlockSpec((sb, tm, c), lambda q: (cur(q) // tps, cur(q) % tps, 0))
    out_tile = pl.BlockSpec((sb, tm, d), lambda q: (prv(q) // tps, prv(q) % tps, 0))
    hist = lambda rows, c: pl.BlockSpec((sb, rows, c), lambda q: (cur(q) // tps, 0, 0))
    vec = lambda c: _resident((None, 1, c), lambda q: (layer, 0, 0))
    return pl.pallas_call(
        functools.partial(_mixout_kernel, tiles_per_seq=tps, n_chunks=MIXOUT_CHUNKS),
        grid=(n_tiles + 1,),
        in_specs=[
            out_tile, conv_tile(d_a), conv_tile(d_b), conv_tile(d_b),
            hist(CONV_A_PAD, d_a), hist(CONV3_PAD, d_b),
            _resident((None, ka, V7X_SUBLANES, d_a), lambda q: (layer, 0, 0, 0)),
            vec(d_a), vec(d_a), vec(d_a),
            _resident((None, kb, V7X_SUBLANES, d_b), lambda q: (layer, 0, 0, 0)),
            _resident((None, d_a + d_b, d), lambda q: (layer, 0, 0)),
        ],
        out_specs=out_tile,
        out_shape=jax.ShapeDtypeStruct((s, t, d), F32),
        scratch_shapes=[
            pltpu.VMEM((sb, d_a // V7X_LANES, CONV_A_PAD + tm, V7X_LANES), F32),
            pltpu.VMEM((sb, d_b // V7X_LANES, CONV3_PAD + tm, V7X_LANES), F32),
            pltpu.VMEM((sb * tm, d_a), F32),
            pltpu.VMEM((sb * tm, d_a + d_b), BF16),
            pltpu.VMEM((sb * tm, d_a + d_b), BF16),
        ],
        compiler_params=_params(1),
        name="mixout",
    )(x, glu, cbh, bb, hist_a, hist_b, conv_a_w, conv_a_b, ln_g, ln_b, conv_b_w, w_out)


def _ffn_kernel(x_ref, g_ref, wg_ref, wv_ref, cwg_ref, cwv_ref, histg_ref, histv_ref,
                wd_ref, fg_ref, o_ref, newg_ref, newv_ref, h_ref, pad0_ref, pad1_ref,
                carry_ref, act_ref, *, n_blocks, final_norm):
    sb, tm, d = x_ref.shape
    fb = wg_ref.shape[-1]
    nc = fb // V7X_LANES
    kf = cwg_ref.shape[0]
    nf = carry_ref.shape[0]
    rows = sb * tm
    p0 = CONV3_PAD - (kf - 1)
    q = pl.program_id(1)
    qa = jnp.minimum(q, n_blocks - 1)
    qb = jnp.maximum(q - 1, 0)
    ta, ja = qa // nf, qa % nf
    jb = qb % nf
    pads = (pad0_ref, pad1_ref)

    @pl.when((q < n_blocks) & (ja == 0))
    def _():
        x = x_ref[...].reshape(rows, d)
        h_ref[...] = _rms_norm_rows(x, g_ref[...]).astype(BF16)

    @pl.when((q > 0) & (jb == 0))
    def _():
        o_ref[...] = x_ref[...]

    def stage_a(pad_ref):
        @pl.when(ta == 0)
        def _():
            for c in range(nc):
                pad_ref[:, c, 0:CONV3_PAD, :] = histg_ref[:, :, _lanes(c)]
                pad_ref[:, nc + c, 0:CONV3_PAD, :] = histv_ref[:, :, _lanes(c)]

        @pl.when(ta > 0)
        def _():
            pad_ref[:, :, 0:CONV3_PAD, :] = carry_ref[ja]

        h = h_ref[...]
        ug = jnp.dot(h, wg_ref[...], preferred_element_type=F32).reshape(sb, tm, fb)
        uv = jnp.dot(h, wv_ref[...], preferred_element_type=F32).reshape(sb, tm, fb)
        for c in range(nc):
            pad_ref[:, c, CONV3_PAD:, :] = ug[:, :, _lanes(c)]
            pad_ref[:, nc + c, CONV3_PAD:, :] = uv[:, :, _lanes(c)]
        carry_ref[ja] = pad_ref[:, :, tm:tm + CONV3_PAD, :]
        for c in range(nc):
            newg_ref[:, :, _lanes(c)] = pad_ref[:, c, tm + p0:tm + CONV3_PAD, :]
            newv_ref[:, :, _lanes(c)] = pad_ref[:, nc + c, tm + p0:tm + CONV3_PAD, :]

    def stage_b(pad_ref):
        rc = min(2 * CONV_ROWS, tm)
        for s_idx in range(sb):
            for r0 in range(0, tm, rc):
                for c in range(nc):
                    cg = _conv_taps(pad_ref, (s_idx, c), r0 + p0, rc, cwg_ref, c * V7X_LANES)
                    cv = _conv_taps(pad_ref, (s_idx, nc + c), r0 + p0, rc, cwv_ref, c * V7X_LANES)
                    row0 = s_idx * tm + r0
                    act_ref[row0:row0 + rc, _lanes(c)] = (cg * _sigmoid(cg) * cv).astype(BF16)
        y = jnp.dot(act_ref[...], wd_ref[...], preferred_element_type=F32)
        o_ref[...] += y.reshape(sb, tm, d)

        if final_norm:
            @pl.when(jb == nf - 1)
            def _():
                o = o_ref[...].reshape(rows, d)
                o_ref[...] = _rms_norm_rows(o, fg_ref[...]).reshape(sb, tm, d)

    @pl.when(q == 0)
    def _():
        stage_a(pads[0])

    for parity in (0, 1):
        @pl.when((q > 0) & (q < n_blocks) & (q % 2 == parity))
        def _(parity=parity):
            stage_a(pads[parity])
            stage_b(pads[1 - parity])

    @pl.when(q == n_blocks)
    def _():
        stage_b(pads[(n_blocks - 1) % 2])


def _ffn(x, norm_g, w_up, conv_w, hist, w_down, final_g, layer, sb, tm, fb, final_norm):
    s, t, d = x.shape
    d_ff = w_down.shape[1]
    kf = conv_w.shape[1]
    nf = d_ff // fb
    assert nf >= 2
    n_blocks = (t // tm) * nf
    qa = lambda q: jnp.minimum(q, n_blocks - 1)
    qb = lambda q: jnp.maximum(q - 1, 0)
    x_tile = pl.BlockSpec((sb, tm, d), lambda i, q: (i, qa(q) // nf, 0))
    o_tile = pl.BlockSpec((sb, tm, d), lambda i, q: (i, qb(q) // nf, 0))
    new_spec = pl.BlockSpec((sb, None, kf - 1, fb), lambda i, q: (i, qa(q) // nf, 0, qa(q) % nf))
    new_sds = jax.ShapeDtypeStruct((s, t // tm, kf - 1, d_ff), F32)
    pad = pltpu.VMEM((sb, 2 * fb // V7X_LANES, CONV3_PAD + tm, V7X_LANES), F32)
    taps = lambda half: pl.BlockSpec((None, kf, V7X_SUBLANES, fb),
                                     lambda i, q: (layer, 0, 0, half * nf + qb(q) % nf))
    x_new, new_g, new_v = pl.pallas_call(
        functools.partial(_ffn_kernel, n_blocks=n_blocks, final_norm=final_norm),
        grid=(s // sb, n_blocks + 1),
        in_specs=[
            x_tile,
            pl.BlockSpec((None, 1, d), lambda i, q: (layer, 0, 0)),
            pl.BlockSpec((None, d, fb), lambda i, q: (layer, 0, qa(q) % nf)),
            pl.BlockSpec((None, d, fb), lambda i, q: (layer, 0, nf + qa(q) % nf)),
            taps(0), taps(1),
            pl.BlockSpec((sb, CONV3_PAD, fb), lambda i, q: (i, 0, qa(q) % nf)),
            pl.BlockSpec((sb, CONV3_PAD, fb), lambda i, q: (i, 0, nf + qa(q) % nf)),
            pl.BlockSpec((None, fb, d), lambda i, q: (layer, qb(q) % nf, 0)),
            pl.BlockSpec((1, d), lambda i, q: (0, 0)),
        ],
        out_specs=[o_tile, new_spec, new_spec],
        out_shape=[jax.ShapeDtypeStruct((s, t, d), F32), new_sds, new_sds],
        scratch_shapes=[
            pltpu.VMEM((sb * tm, d), BF16),
            pad, pad,
            pltpu.VMEM((nf, sb, 2 * fb // V7X_LANES, CONV3_PAD, V7X_LANES), F32),
            pltpu.VMEM((sb * tm, fb), BF16),
        ],
        compiler_params=_params(2),
        name="convffn",
    )(x, norm_g, w_up, w_up, conv_w, conv_w, hist, hist, w_down, final_g)
    return x_new, jnp.concatenate([new_g[:, -1], new_v[:, -1]], axis=-1)


def _tiling(s, t):
    tm = min(t, 512)
    sb = s if s * t <= 512 else 1
    return sb, tm


def _front_pad(state, rows):
    return jnp.pad(state, ((0, 0), (0, 0), (rows - state.shape[2], 0), (0, 0)))


def _trunk(x, state_a, state_b, state_f, p):
    s, t, d = x.shape
    sb, tm = _tiling(s, t)
    tm_in = min(t, 2 * tm) if sb == 1 else tm
    depth = p["w_in"].shape[0]
    ka1, kb1 = state_a.shape[2], state_b.shape[2]
    hist_a = _front_pad(state_a, CONV_A_PAD)
    hist_b = _front_pad(state_b, CONV3_PAD)
    hist_f = _front_pad(state_f, CONV3_PAD)
    new_a, new_b, new_f = [], [], []
    for layer in range(depth):
        glu, cbh, bb = _inproj(x, p["norm1_g"], p["w_in"], layer, sb, tm_in, cb=256)
        new_a.append(glu[:, t - ka1:, :])
        new_b.append(cbh[:, t - kb1:, :])
        x = _mixout(x, glu, cbh, bb, hist_a[layer], hist_b[layer],
                    p["conv_a_w"], p["conv_a_b"], p["ln_a_g"], p["ln_a_b"],
                    p["conv_b_w"], p["w_out"], layer, sb, tm)
        x, nf = _ffn(x, p["norm2_g"], p["w_up"], p["conv_ffn_w"], hist_f[layer],
                     p["w_down"], p["final_g"], layer, sb, tm, fb=512,
                     final_norm=(layer == depth - 1))
        new_f.append(nf)
    return x, jnp.stack(new_a), jnp.stack(new_b), jnp.stack(new_f)


def kernel(x_prompt, x_sample, state_conv_a, state_conv_b, state_ffn, norm1_g, w_in,
           conv_a_w, conv_a_b, ln_a_g, ln_a_b, conv_b_w, w_out, norm2_g, w_up,
           conv_ffn_w, w_down, final_g):
    depth = w_in.shape[0]
    vec = lambda a: a.reshape(depth, 1, a.shape[-1])
    taps = lambda w: jnp.broadcast_to(w[:, :, None, :], w.shape[:2] + (V7X_SUBLANES, w.shape[2]))
    p = dict(
        norm1_g=vec(norm1_g), w_in=w_in.astype(BF16),
        conv_a_w=taps(conv_a_w), conv_a_b=vec(conv_a_b), ln_a_g=vec(ln_a_g), ln_a_b=vec(ln_a_b),
        conv_b_w=taps(conv_b_w), w_out=w_out.astype(BF16),
        norm2_g=vec(norm2_g), w_up=w_up.astype(BF16), conv_ffn_w=taps(conv_ffn_w),
        w_down=w_down.astype(BF16), final_g=final_g.reshape(1, -1),
    )
    bp = x_prompt.shape[0]
    no_history = lambda st: jnp.zeros((depth, bp) + st.shape[2:], st.dtype)
    yp, pa, pb, pf = _trunk(x_prompt, no_history(state_conv_a), no_history(state_conv_b),
                            no_history(state_ffn), p)
    ys, sa, sbb, sf = _trunk(x_sample, state_conv_a, state_conv_b, state_ffn, p)
    return yp, ys, pa, pb, pf, sa, sbb, sf
```

```python
import functools

import jax
import jax.numpy as jnp
from jax import lax
from jax.experimental import pallas as pl
from jax.experimental.pallas import tpu as pltpu

F32 = jnp.float32
BF16 = jnp.bfloat16
EPS = 1e-6

V7X_SUBLANES = 8
V7X_LANES = 128
V7X_VMEM_LIMIT_BYTES = 56 * 1024 * 1024

CONV_A_PAD = 32
CONV3_PAD = V7X_SUBLANES
CONV_ROWS = 32
NORM_ROWS = 16
NORM_UNROLL = 16
CONV_BLOCK_ROWS = 128
MIXOUT_CHUNKS = 2


def _sigmoid(x):
    return 1.0 / (1.0 + jnp.exp(-x))


def _rms_norm_rows(x, g):
    y = x * lax.rsqrt(jnp.mean(x * x, axis=-1, keepdims=True) + EPS)
    return y * g


def _lanes(c):
    return slice(c * V7X_LANES, (c + 1) * V7X_LANES)


def _params(n_axes):
    return pltpu.CompilerParams(
        dimension_semantics=("arbitrary",) * n_axes,
        vmem_limit_bytes=V7X_VMEM_LIMIT_BYTES,
    )


def _resident(block_shape, index_map):
    return pl.BlockSpec(block_shape, index_map, pipeline_mode=pl.Buffered(1))


def _conv_taps(pad_ref, lead, first_row, rows, w_ref, lane0):
    sub = V7X_SUBLANES
    groups = rows // sub
    taps = w_ref.shape[0]
    acc = [None] * groups
    for r in range(min(sub, taps)):
        ws = [w_ref[k, :, pl.ds(lane0, V7X_LANES)] for k in range(r, taps, sub)]
        for m in range(groups + len(ws) - 1):
            win = pad_ref[lead + (pl.ds(first_row + sub * m + r, sub), slice(None))]
            for i, w in enumerate(ws):
                g = m - i
                if 0 <= g < groups:
                    term = win * w
                    acc[g] = term if acc[g] is None else acc[g] + term
    return jnp.concatenate(acc, axis=0)


def _inproj_kernel(x_ref, g_ref, wav_ref, wag_ref, wbb_ref, wbc_ref, wbh_ref,
                   glu_ref, cbh_ref, bb_ref, h_ref):
    sb, tm, d = x_ref.shape
    cb = glu_ref.shape[-1]
    rows = sb * tm

    @pl.when(pl.program_id(2) == 0)
    def _():
        x = x_ref[...].reshape(rows, d)
        h_ref[...] = _rms_norm_rows(x, g_ref[...]).astype(BF16)

    h = h_ref[...]

    def proj(w_ref):
        return jnp.dot(h, w_ref[...], preferred_element_type=F32)

    glu = proj(wav_ref) * _sigmoid(proj(wag_ref))
    glu_ref[...] = glu.reshape(sb, tm, cb)
    cbh_ref[...] = (proj(wbc_ref) * proj(wbh_ref)).reshape(sb, tm, cb)
    bb_ref[...] = proj(wbb_ref).reshape(sb, tm, cb)


def _inproj(x, norm_g, w_in, layer, sb, tm, cb):
    s, t, d = x.shape
    d_a = w_in.shape[-1] // 5
    nj = d_a // cb
    grid = (s // sb, t // tm, nj)

    def wspec(group):
        return pl.BlockSpec((None, d, cb), lambda i, k, j: (layer, 0, group * nj + j))

    act_spec = pl.BlockSpec((sb, tm, cb), lambda i, k, j: (i, k, j))
    out_sds = jax.ShapeDtypeStruct((s, t, d_a), F32)
    return pl.pallas_call(
        _inproj_kernel,
        grid=grid,
        in_specs=[
            pl.BlockSpec((sb, tm, d), lambda i, k, j: (i, k, 0)),
            pl.BlockSpec((None, 1, d), lambda i, k, j: (layer, 0, 0)),
            wspec(0), wspec(1), wspec(2), wspec(3), wspec(4),
        ],
        out_specs=[act_spec, act_spec, act_spec],
        out_shape=[out_sds, out_sds, out_sds],
        scratch_shapes=[pltpu.VMEM((sb * tm, d), BF16)],
        compiler_params=_params(3),
        name="inproj",
    )(x, norm_g, w_in, w_in, w_in, w_in, w_in)


def _mixout_kernel(x_ref, glu_ref, cbh_ref, bb_ref, hista_ref, histb_ref,
                   wa_ref, ba_ref, lng_ref, lnb_ref, wb_ref, wout_ref,
                   o_ref, pada_ref, padb_ref, conva_ref, mix0_ref, mix1_ref,
                   *, tiles_per_seq, n_chunks):
    sb, tm, d = x_ref.shape
    d_a = glu_ref.shape[-1]
    d_b = cbh_ref.shape[-1]
    nca = d_a // V7X_LANES
    ncb = d_b // V7X_LANES
    a0 = CONV_A_PAD - (wa_ref.shape[0] - 1)
    b0 = CONV3_PAD - (wb_ref.shape[0] - 1)
    rc = min(NORM_ROWS, tm)
    cr = min(CONV_BLOCK_ROWS, tm)
    dn = d // n_chunks
    q = pl.program_id(0)

    @pl.when(q == 0)
    def _():
        mix1_ref[...] = jnp.zeros(mix1_ref.shape, mix1_ref.dtype)

    @pl.when(q % tiles_per_seq == 0)
    def _():
        for c in range(nca):
            pada_ref[:, c, 0:CONV_A_PAD, :] = hista_ref[:, :, _lanes(c)]
        for c in range(ncb):
            padb_ref[:, c, 0:CONV3_PAD, :] = histb_ref[:, :, _lanes(c)]

    for c in range(nca):
        pada_ref[:, c, CONV_A_PAD:, :] = glu_ref[:, :, _lanes(c)]
    for c in range(ncb):
        padb_ref[:, c, CONV3_PAD:, :] = cbh_ref[:, :, _lanes(c)]

    def step(mix_cur, mix_prev):
        def chunk(i, carry):
            col0 = pl.multiple_of(i * dn, dn)
            y = jnp.dot(mix_prev[...], wout_ref[:, pl.ds(col0, dn)], preferred_element_type=F32)
            o_ref[:, :, pl.ds(col0, dn)] = x_ref[:, :, pl.ds(col0, dn)] + y.reshape(sb, tm, dn)
            for cc in range(nca // n_chunks):
                c = i * (nca // n_chunks) + cc
                lane0 = pl.multiple_of(c * V7X_LANES, V7X_LANES)
                for s_idx in range(sb):
                    for r0 in range(0, tm, cr):
                        row0 = s_idx * tm + r0
                        conva_ref[row0:row0 + cr, pl.ds(lane0, V7X_LANES)] = _conv_taps(
                            pada_ref, (s_idx, c), r0 + a0, cr, wa_ref, lane0)
            for cc in range(ncb // n_chunks):
                c = i * (ncb // n_chunks) + cc
                lane0 = pl.multiple_of(c * V7X_LANES, V7X_LANES)
                for s_idx in range(sb):
                    for r0 in range(0, tm, cr):
                        row0 = s_idx * tm + r0
                        u = _conv_taps(padb_ref, (s_idx, c), r0 + b0, cr, wb_ref, lane0)
                        yb = bb_ref[s_idx, r0:r0 + cr, pl.ds(lane0, V7X_LANES)] * u
                        mix_cur[row0:row0 + cr, pl.ds(d_a + lane0, V7X_LANES)] = yb.astype(BF16)
            return carry

        lax.fori_loop(0, n_chunks, chunk, 0)

        def norm(i, carry):
            row0 = pl.multiple_of(i * rc, rc)
            y = conva_ref[pl.ds(row0, rc), :] + ba_ref[...]
            mu = jnp.mean(y, axis=-1, keepdims=True)
            yc = y - mu
            var = jnp.mean(yc * yc, axis=-1, keepdims=True)
            yn = yc * lax.rsqrt(var + EPS) * lng_ref[...] + lnb_ref[...]
            mix_cur[pl.ds(row0, rc), 0:d_a] = (yn * _sigmoid(yn)).astype(BF16)
            return carry

        lax.fori_loop(0, sb * tm // rc, norm, 0, unroll=NORM_UNROLL)

    for parity, (cur, prev) in enumerate(((mix0_ref, mix1_ref), (mix1_ref, mix0_ref))):
        @pl.when(q % 2 == parity)
        def _(cur=cur, prev=prev):
            step(cur, prev)

    pada_ref[:, :, 0:CONV_A_PAD, :] = pada_ref[:, :, tm:tm + CONV_A_PAD, :]
    padb_ref[:, :, 0:CONV3_PAD, :] = padb_ref[:, :, tm:tm + CONV3_PAD, :]


def _mixout(x, glu, cbh, bb, hist_a, hist_b, conv_a_w, conv_a_b, ln_g, ln_b,
            conv_b_w, w_out, layer, sb, tm):
    s, t, d = x.shape
    d_a = glu.shape[-1]
    d_b = cbh.shape[-1]
    ka = conv_a_w.shape[1]
    kb = conv_b_w.shape[1]
    tps = t // tm
    n_tiles = (s // sb) * tps
    cur = lambda q: jnp.minimum(q, n_tiles - 1)
    prv = lambda q: jnp.maximum(q - 1, 0)
    conv_tile = lambda c: pl.BlockSpec((sb, tm, c), lambda q: (cur(q) // tps, cur(q) % tps, 0))
    out_tile = pl.BlockSpec((sb, tm, d), lambda q: (prv(q) // tps, prv(q) % tps, 0))
    hist = lambda rows, c: pl.BlockSpec((sb, rows, c), lambda q: (cur(q) // tps, 0, 0))
    vec = lambda c: _resident((None, 1, c), lambda q: (layer, 0, 0))
    return pl.pallas_call(
        functools.partial(_mixout_kernel, tiles_per_seq=tps, n_chunks=MIXOUT_CHUNKS),
        grid=(n_tiles + 1,),
        in_specs=[
            out_tile, conv_tile(d_a), conv_tile(d_b), conv_tile(d_b),
            hist(CONV_A_PAD, d_a), hist(CONV3_PAD, d_b),
            _resident((None, ka, V7X_SUBLANES, d_a), lambda q: (layer, 0, 0, 0)),
            vec(d_a), vec(d_a), vec(d_a),
            _resident((None, kb, V7X_SUBLANES, d_b), lambda q: (layer, 0, 0, 0)),
            _resident((None, d_a + d_b, d), lambda q: (layer, 0, 0)),
        ],
        out_specs=out_tile,
        out_shape=jax.ShapeDtypeStruct((s, t, d), F32),
        scratch_shapes=[
            pltpu.VMEM((sb, d_a // V7X_LANES, CONV_A_PAD + tm, V7X_LANES), F32),
            pltpu.VMEM((sb, d_b // V7X_LANES, CONV3_PAD + tm, V7X_LANES), F32),
            pltpu.VMEM((sb * tm, d_a), F32),
            pltpu.VMEM((sb * tm, d_a + d_b), BF16),
            pltpu.VMEM((sb * tm, d_a + d_b), BF16),
        ],
        compiler_params=_params(1),
        name="mixout",
    )(x, glu, cbh, bb, hist_a, hist_b, conv_a_w, conv_a_b, ln_g, ln_b, conv_b_w, w_out)


def _ffn_kernel(x_ref, g_ref, wg_ref, wv_ref, cwg_ref, cwv_ref, histg_ref, histv_ref,
                wd_ref, fg_ref, o_ref, newg_ref, newv_ref, h_ref, pad0_ref, pad1_ref,
                carry_ref, act_ref, *, n_blocks, final_norm):
    sb, tm, d = x_ref.shape
    fb = wg_ref.shape[-1]
    nc = fb // V7X_LANES
    kf = cwg_ref.shape[0]
    nf = carry_ref.shape[0]
    rows = sb * tm
    p0 = CONV3_PAD - (kf - 1)
    q = pl.program_id(1)
    qa = jnp.minimum(q, n_blocks - 1)
    qb = jnp.maximum(q - 1, 0)
    ta, ja = qa // nf, qa % nf
    jb = qb % nf
    pads = (pad0_ref, pad1_ref)

    @pl.when((q < n_blocks) & (ja == 0))
    def _():
        x = x_ref[...].reshape(rows, d)
        h_ref[...] = _rms_norm_rows(x, g_ref[...]).astype(BF16)

    @pl.when((q > 0) & (jb == 0))
    def _():
        o_ref[...] = x_ref[...]

    def stage_a(pad_ref):
        @pl.when(ta == 0)
        def _():
            for c in range(nc):
                pad_ref[:, c, 0:CONV3_PAD, :] = histg_ref[:, :, _lanes(c)]
                pad_ref[:, nc + c, 0:CONV3_PAD, :] = histv_ref[:, :, _lanes(c)]

        @pl.when(ta > 0)
        def _():
            pad_ref[:, :, 0:CONV3_PAD, :] = carry_ref[ja]

        h = h_ref[...]
        ug = jnp.dot(h, wg_ref[...], preferred_element_type=F32).reshape(sb, tm, fb)
        uv = jnp.dot(h, wv_ref[...], preferred_element_type=F32).reshape(sb, tm, fb)
        for c in range(nc):
            pad_ref[:, c, CONV3_PAD:, :] = ug[:, :, _lanes(c)]
            pad_ref[:, nc + c, CONV3_PAD:, :] = uv[:, :, _lanes(c)]
        carry_ref[ja] = pad_ref[:, :, tm:tm + CONV3_PAD, :]
        for c in range(nc):
            newg_ref[:, :, _lanes(c)] = pad_ref[:, c, tm + p0:tm + CONV3_PAD, :]
            newv_ref[:, :, _lanes(c)] = pad_ref[:, nc + c, tm + p0:tm + CONV3_PAD, :]

    def stage_b(pad_ref):
        rc = min(2 * CONV_ROWS, tm)
        for s_idx in range(sb):
            for r0 in range(0, tm, rc):
                for c in range(nc):
                    cg = _conv_taps(pad_ref, (s_idx, c), r0 + p0, rc, cwg_ref, c * V7X_LANES)
                    cv = _conv_taps(pad_ref, (s_idx, nc + c), r0 + p0, rc, cwv_ref, c * V7X_LANES)
                    row0 = s_idx * tm + r0
                    act_ref[row0:row0 + rc, _lanes(c)] = (cg * _sigmoid(cg) * cv).astype(BF16)
        y = jnp.dot(act_ref[...], wd_ref[...], preferred_element_type=F32)
        o_ref[...] += y.reshape(sb, tm, d)

        if final_norm:
            @pl.when(jb == nf - 1)
            def _():
                o = o_ref[...].reshape(rows, d)
                o_ref[...] = _rms_norm_rows(o, fg_ref[...]).reshape(sb, tm, d)

    @pl.when(q == 0)
    def _():
        stage_a(pads[0])

    for parity in (0, 1):
        @pl.when((q > 0) & (q < n_blocks) & (q % 2 == parity))
        def _(parity=parity):
            stage_a(pads[parity])
            stage_b(pads[1 - parity])

    @pl.when(q == n_blocks)
    def _():
        stage_b(pads[(n_blocks - 1) % 2])


def _ffn(x, norm_g, w_up, conv_w, hist, w_down, final_g, layer, sb, tm, fb, final_norm):
    s, t, d = x.shape
    d_ff = w_down.shape[1]
    kf = conv_w.shape[1]
    nf = d_ff // fb
    assert nf >= 2
    n_blocks = (t // tm) * nf
    qa = lambda q: jnp.minimum(q, n_blocks - 1)
    qb = lambda q: jnp.maximum(q - 1, 0)
    x_tile = pl.BlockSpec((sb, tm, d), lambda i, q: (i, qa(q) // nf, 0),
                          pipeline_mode=pl.Buffered(1))
    o_tile = pl.BlockSpec((sb, tm, d), lambda i, q: (i, qb(q) // nf, 0))
    new_spec = pl.BlockSpec((sb, None, kf - 1, fb), lambda i, q: (i, qa(q) // nf, 0, qa(q) % nf))
    new_sds = jax.ShapeDtypeStruct((s, t // tm, kf - 1, d_ff), F32)
    pad = pltpu.VMEM((sb, 2 * fb // V7X_LANES, CONV3_PAD + tm, V7X_LANES), F32)
    taps = lambda half: pl.BlockSpec((None, kf, V7X_SUBLANES, fb),
                                     lambda i, q: (layer, 0, 0, half * nf + qb(q) % nf))
    x_new, new_g, new_v = pl.pallas_call(
        functools.partial(_ffn_kernel, n_blocks=n_blocks, final_norm=final_norm),
        grid=(s // sb, n_blocks + 1),
        in_specs=[
            x_tile,
            pl.BlockSpec((None, 1, d), lambda i, q: (layer, 0, 0)),
            pl.BlockSpec((None, d, fb), lambda i, q: (layer, 0, qa(q) % nf)),
            pl.BlockSpec((None, d, fb), lambda i, q: (layer, 0, nf + qa(q) % nf)),
            taps(0), taps(1),
            pl.BlockSpec((sb, CONV3_PAD, fb), lambda i, q: (i, 0, qa(q) % nf)),
            pl.BlockSpec((sb, CONV3_PAD, fb), lambda i, q: (i, 0, nf + qa(q) % nf)),
            pl.BlockSpec((None, fb, d), lambda i, q: (layer, qb(q) % nf, 0)),
            pl.BlockSpec((1, d), lambda i, q: (0, 0)),
        ],
        out_specs=[o_tile, new_spec, new_spec],
        out_shape=[jax.ShapeDtypeStruct((s, t, d), F32), new_sds, new_sds],
        scratch_shapes=[
            pltpu.VMEM((sb * tm, d), BF16),
            pad, pad,
            pltpu.VMEM((nf, sb, 2 * fb // V7X_LANES, CONV3_PAD, V7X_LANES), F32),
            pltpu.VMEM((sb * tm, fb), BF16),
        ],
        compiler_params=_params(2),
        name="convffn",
    )(x, norm_g, w_up, w_up, conv_w, conv_w, hist, hist, w_down, final_g)
    return x_new, jnp.concatenate([new_g[:, -1], new_v[:, -1]], axis=-1)


def _tiling(s, t):
    tm = min(t, 512)
    sb = s if s * t <= 512 else 1
    return sb, tm


def _front_pad(state, rows):
    return jnp.pad(state, ((0, 0), (0, 0), (rows - state.shape[2], 0), (0, 0)))


def _trunk(x, state_a, state_b, state_f, p):
    s, t, d = x.shape
    sb, tm = _tiling(s, t)
    tm_in = min(t, 2 * tm) if sb == 1 else tm
    tm_ffn = tm_in
    depth = p["w_in"].shape[0]
    ka1, kb1 = state_a.shape[2], state_b.shape[2]
    hist_a = _front_pad(state_a, CONV_A_PAD)
    hist_b = _front_pad(state_b, CONV3_PAD)
    hist_f = _front_pad(state_f, CONV3_PAD)
    new_a, new_b, new_f = [], [], []
    for layer in range(depth):
        glu, cbh, bb = _inproj(x, p["norm1_g"], p["w_in"], layer, sb, tm_in, cb=256)
        new_a.append(glu[:, t - ka1:, :])
        new_b.append(cbh[:, t - kb1:, :])
        x = _mixout(x, glu, cbh, bb, hist_a[layer], hist_b[layer],
                    p["conv_a_w"], p["conv_a_b"], p["ln_a_g"], p["ln_a_b"],
                    p["conv_b_w"], p["w_out"], layer, sb, tm)
        x, nf = _ffn(x, p["norm2_g"], p["w_up"], p["conv_ffn_w"], hist_f[layer],
                     p["w_down"], p["final_g"], layer, sb, tm_ffn, fb=512,
                     final_norm=(layer == depth - 1))
        new_f.append(nf)
    return x, jnp.stack(new_a), jnp.stack(new_b), jnp.stack(new_f)


def kernel(x_prompt, x_sample, state_conv_a, state_conv_b, state_ffn, norm1_g, w_in,
           conv_a_w, conv_a_b, ln_a_g, ln_a_b, conv_b_w, w_out, norm2_g, w_up,
           conv_ffn_w, w_down, final_g):
    depth = w_in.shape[0]
    vec = lambda a: a.reshape(depth, 1, a.shape[-1])
    taps = lambda w: jnp.broadcast_to(w[:, :, None, :], w.shape[:2] + (V7X_SUBLANES, w.shape[2]))
    p = dict(
        norm1_g=vec(norm1_g), w_in=w_in.astype(BF16),
        conv_a_w=taps(conv_a_w), conv_a_b=vec(conv_a_b), ln_a_g=vec(ln_a_g), ln_a_b=vec(ln_a_b),
        conv_b_w=taps(conv_b_w), w_out=w_out.astype(BF16),
        norm2_g=vec(norm2_g), w_up=w_up.astype(BF16), conv_ffn_w=taps(conv_ffn_w),
        w_down=w_down.astype(BF16), final_g=final_g.reshape(1, -1),
    )
    bp = x_prompt.shape[0]
    no_history = lambda st: jnp.zeros((depth, bp) + st.shape[2:], st.dtype)
    yp, pa, pb, pf = _trunk(x_prompt, no_history(state_conv_a), no_history(state_conv_b),
                            no_history(state_ffn), p)
    ys, sa, sbb, sf = _trunk(x_sample, state_conv_a, state_conv_b, state_ffn, p)
    return yp, ys, pa, pb, pf, sa, sbb, sf
```
